```python
import math
import jax, jax.numpy as jnp
from jax import lax
import numpy as np

D_MODEL = 1024
BATCH = 2
SEQ = 8192
DEPTH = 2

CHUNK = 64
Q_BLOCK = 128
ROPE_THETA = 500000.0
EPS = 1e-6

GLA_HEADS = 4
GLA_KEY_DIM = D_MODEL // 2
GLA_VAL_DIM = D_MODEL
GLA_DK = GLA_KEY_DIM // GLA_HEADS
GLA_DV = GLA_VAL_DIM // GLA_HEADS
GLA_LOW_RANK = 16
GLA_GATE_NORMALIZER = 16.0

SSM_D_INNER = 2 * D_MODEL
SSM_HEAD_DIM = 64
SSM_HEADS = SSM_D_INNER // SSM_HEAD_DIM
SSM_GROUPS = 4
SSM_HEADS_PER_GROUP = SSM_HEADS // SSM_GROUPS
SSM_STATE = 128
SSM_CONV_W = 4
SSM_BC_WIDTH = SSM_GROUPS * SSM_STATE
SSM_CONV_DIM = SSM_D_INNER + 2 * SSM_BC_WIDTH

DIFF_HEADS = 8
DIFF_HEAD_DIM = 64
DIFF_V_DIM = 2 * DIFF_HEAD_DIM
DIFF_QK_WIDTH = DIFF_HEADS * 2 * DIFF_HEAD_DIM
DIFF_V_WIDTH = DIFF_HEADS * DIFF_V_DIM
ROT_DIM = DIFF_HEAD_DIM // 4
DIFF_SUBLN_EPS = 1e-5

N_BRANCHES = 3
D_FF = 4 * D_MODEL

IN_SPLITS = (GLA_KEY_DIM, GLA_KEY_DIM, GLA_VAL_DIM, GLA_LOW_RANK, GLA_VAL_DIM,
             SSM_D_INNER, SSM_CONV_DIM, SSM_HEADS,
             DIFF_QK_WIDTH, DIFF_QK_WIDTH, DIFF_V_WIDTH,
             N_BRANCHES * D_MODEL)
IN_COLS = (GLA_KEY_DIM + GLA_KEY_DIM + GLA_VAL_DIM + GLA_LOW_RANK + GLA_VAL_DIM
           + SSM_D_INNER + SSM_CONV_DIM + SSM_HEADS
           + DIFF_QK_WIDTH + DIFF_QK_WIDTH + DIFF_V_WIDTH + N_BRANCHES * D_MODEL)

kernel_name = 'hybrid_gated_gla_ssd_diffattn_trunk'

F32 = jnp.float32


def _rms(x, eps=EPS):
    xf = x.astype(F32)
    return xf * lax.rsqrt(jnp.mean(xf * xf, axis=-1, keepdims=True) + eps)


def rmsnorm(x, g):
    return (_rms(x) * g.astype(F32)).astype(x.dtype)


def _split_points():
    pts = []
    acc = 0
    for s in IN_SPLITS[:-1]:
        acc += s
        pts.append(acc)
    return pts


def segsum(t):
    T = t.shape[-1]
    tr = jnp.broadcast_to(t[..., :, None], t.shape + (T,))
    strict = jnp.tril(jnp.ones((T, T), bool), -1)
    cs = jnp.cumsum(jnp.where(strict, tr, 0.0), axis=-2)
    return jnp.where(jnp.tril(jnp.ones((T, T), bool)), cs, -jnp.inf)


def gla_mixer(q, k, v, gk_low, g_out, w_gk2, b_gk, norm_g):
    bsz, s_len, _ = q.shape
    nc = s_len // CHUNK
    gk = jax.nn.log_sigmoid(gk_low.astype(F32) @ w_gk2.astype(F32) + b_gk.astype(F32)) / GLA_GATE_NORMALIZER

    def to_chunks(t, d):
        return t.astype(F32).reshape(bsz, nc, CHUNK, GLA_HEADS, d).transpose(1, 0, 3, 2, 4)

    qc = to_chunks(q, GLA_DK) * (GLA_DK ** -0.5)
    kc = to_chunks(k, GLA_DK)
    vc = to_chunks(v, GLA_DV)
    bc = jnp.cumsum(to_chunks(gk, GLA_DK), axis=-2)
    causal = jnp.tril(jnp.ones((CHUNK, CHUNK), bool))

    def step(state, inp):
        qi, ki, vi, bi = inp
        o_inter = jnp.einsum('bhcd,bhde->bhce', qi * jnp.exp(bi), state)
        rel = bi[:, :, :, None, :] - bi[:, :, None, :, :]
        decay = jnp.exp(jnp.where(causal[:, :, None], rel, -jnp.inf))
        scores = jnp.einsum('bhid,bhjd,bhijd->bhij', qi, ki, decay)
        o = o_inter + jnp.einsum('bhij,bhje->bhie', scores, vi)
        b_last = bi[:, :, -1:, :]
        state = state * jnp.exp(b_last)[:, :, 0, :, None] + jnp.einsum(
            'bhcd,bhce->bhde', ki * jnp.exp(b_last - bi), vi)
        return state, o

    s0 = jnp.zeros((bsz, GLA_HEADS, GLA_DK, GLA_DV), F32)
    _, o = lax.scan(step, s0, (qc, kc, vc, bc))
    o = o.transpose(1, 0, 3, 2, 4).reshape(bsz, s_len, GLA_HEADS, GLA_DV)
    g = g_out.astype(F32).reshape(bsz, s_len, GLA_HEADS, GLA_DV)
    o = _rms(o) * norm_g.astype(F32) * jax.nn.silu(g)
    return o.reshape(bsz, s_len, GLA_VAL_DIM).astype(q.dtype)


def ssd_mixer(z, xbc, dt_raw, conv_w, conv_b, dt_bias, a_log, d_skip, norm_g):
    bsz, s_len, _ = z.shape
    nc = s_len // CHUNK
    G, R, P, N = SSM_GROUPS, SSM_HEADS_PER_GROUP, SSM_HEAD_DIM, SSM_STATE
    xp = jnp.pad(xbc.astype(F32), ((0, 0), (SSM_CONV_W - 1, 0), (0, 0)))
    conv = conv_b.astype(F32)
    for i in range(SSM_CONV_W):
        conv = conv + xp[:, i:i + s_len, :] * conv_w[i].astype(F32)
    xbc = jax.nn.silu(conv)
    xs = xbc[..., :SSM_D_INNER]
    bm = xbc[..., SSM_D_INNER:SSM_D_INNER + SSM_BC_WIDTH]
    cm = xbc[..., SSM_D_INNER + SSM_BC_WIDTH:]
    dt = jax.nn.softplus(dt_raw.astype(F32) + dt_bias.astype(F32))
    a = -jnp.exp(a_log.astype(F32))

    x_c = xs.reshape(bsz, nc, CHUNK, G, R, P)
    dtc = dt.reshape(bsz, nc, CHUNK, G, R)
    xdt = x_c * dtc[..., None]
    bc = bm.reshape(bsz, nc, CHUNK, G, N)
    cc = cm.reshape(bsz, nc, CHUNK, G, N)
    da = (dtc * a.reshape(G, R)).transpose(0, 3, 4, 1, 2)
    a_cum = jnp.cumsum(da, axis=-1)

    lmat = jnp.exp(segsum(da))
    y_diag = jnp.einsum('bclgn,bcsgn,bgrcls,bcsgrp->bclgrp', cc, bc, lmat, xdt)
    decay_states = jnp.exp(a_cum[..., -1:] - a_cum)
    states = jnp.einsum('bclgn,bgrcl,bclgrp->bcgrpn', bc, decay_states, xdt)
    states = jnp.concatenate([jnp.zeros_like(states[:, :1]), states], axis=1)
    chunk_decay = jnp.exp(segsum(jnp.pad(a_cum[..., -1], ((0, 0), (0, 0), (0, 0), (1, 0)))))
    states = jnp.einsum('bgrzc,bcgrpn->bzgrpn', chunk_decay, states)[:, :-1]
    y_off = jnp.einsum('bclgn,bcgrpn,bgrcl->bclgrp', cc, states, jnp.exp(a_cum))
    y = y_diag + y_off + x_c * d_skip.astype(F32).reshape(G, R)[:, :, None]
    y = y.reshape(bsz, s_len, SSM_D_INNER) * jax.nn.silu(z.astype(F32))
    y = _rms(y.reshape(bsz, s_len, G, SSM_D_INNER // G)).reshape(bsz, s_len, SSM_D_INNER)
    return (y * norm_g.astype(F32)).astype(z.dtype)


def partial_rope(t, cos, sin):
    half = ROT_DIM // 2
    t1 = t[..., :half]
    t2 = t[..., half:ROT_DIM]
    return jnp.concatenate([t1 * cos - t2 * sin, t2 * cos + t1 * sin, t[..., ROT_DIM:]], axis=-1)


def diff_mixer(q, k, v, positions, lq1, lk1, lq2, lk2, norm_g, lambda_init):
    bsz, s_len, _ = q.shape
    nb = s_len // Q_BLOCK
    q = q.astype(F32).reshape(bsz, s_len, DIFF_HEADS, 2, DIFF_HEAD_DIM)
    k = k.astype(F32).reshape(bsz, s_len, DIFF_HEADS, 2, DIFF_HEAD_DIM)
    v = v.astype(F32).reshape(bsz, s_len, DIFF_HEADS, DIFF_V_DIM)
    inv_freq = ROPE_THETA ** (-jnp.arange(0, ROT_DIM, 2, dtype=F32) / ROT_DIM)
    ang = positions.astype(F32)[..., None] * inv_freq
    cos = jnp.cos(ang)[:, :, None, None, :]
    sin = jnp.sin(ang)[:, :, None, None, :]
    q = partial_rope(q, cos, sin)
    k = partial_rope(k, cos, sin)
    lam = (jnp.exp(jnp.sum(lq1.astype(F32) * lk1.astype(F32)))
           - jnp.exp(jnp.sum(lq2.astype(F32) * lk2.astype(F32))) + lambda_init)

    qb = q.reshape(bsz, nb, Q_BLOCK, DIFF_HEADS, 2, DIFF_HEAD_DIM).transpose(1, 0, 3, 4, 2, 5)
    kt = k.transpose(0, 2, 3, 1, 4)
    vt = v.transpose(0, 2, 1, 3)
    key_chunk = jnp.arange(s_len) // CHUNK
    scale = DIFF_HEAD_DIM ** -0.5

    def attend(args):
        q_blk, blk = args
        q_chunk = (blk * Q_BLOCK + jnp.arange(Q_BLOCK)) // CHUNK
        allowed = key_chunk[None, :] <= q_chunk[:, None]
        s = jnp.einsum('bhtqd,bhtkd->bhtqk', q_blk, kt) * scale
        p = jax.nn.softmax(jnp.where(allowed, s, -jnp.inf), axis=-1)
        w = p[:, :, 0] - lam * p[:, :, 1]
        return jnp.einsum('bhqk,bhkd->bhqd', w, vt)

    o = lax.map(attend, (qb, jnp.arange(nb)))
    o = o.transpose(1, 0, 3, 2, 4).reshape(bsz, s_len, DIFF_HEADS, DIFF_V_DIM)
    o = _rms(o, DIFF_SUBLN_EPS) * norm_g.astype(F32) * (1.0 - lambda_init)
    return o.reshape(bsz, s_len, DIFF_V_WIDTH).astype(positions.dtype if False else jnp.result_type(norm_g))


def setup_inputs(seed: int = 0) -> dict:
    key = jax.random.key(seed)
    ks = jax.random.split(key, 32)

    def nrm(k, shape, scale):
        return jax.random.normal(k, shape, F32) * scale

    x = jax.random.normal(ks[0], (BATCH, SEQ, D_MODEL), F32)
    offset = jax.random.randint(ks[1], (BATCH,), 0, 4096, dtype=jnp.int32)
    positions = offset[:, None] + jnp.arange(SEQ, dtype=jnp.int32)[None, :]
    dt = jnp.exp(jax.random.uniform(ks[10], (DEPTH, SSM_HEADS), F32, math.log(1e-3), math.log(1e-1)))
    return {
        'x': x,
        'positions': positions,
        'norm_mix_g': 1.0 + nrm(ks[2], (DEPTH, D_MODEL), 0.01),
        'w_in': nrm(ks[3], (DEPTH, D_MODEL, IN_COLS), D_MODEL ** -0.5),
        'b_gate': nrm(ks[4], (DEPTH, N_BRANCHES * D_MODEL), 0.01),
        'gla_w_gk2': nrm(ks[5], (DEPTH, GLA_LOW_RANK, GLA_KEY_DIM), GLA_LOW_RANK ** -0.5),
        'gla_b_gk': nrm(ks[6], (DEPTH, GLA_KEY_DIM), 0.01),
        'gla_norm_g': 1.0 + nrm(ks[7], (DEPTH, GLA_DV), 0.01),
        'ssm_conv_w': nrm(ks[8], (DEPTH, SSM_CONV_W, SSM_CONV_DIM), SSM_CONV_W ** -0.5),
        'ssm_conv_b': nrm(ks[9], (DEPTH, SSM_CONV_DIM), 0.01),
        'ssm_dt_bias': dt + jnp.log(-jnp.expm1(-dt)),
        'ssm_a_log': jnp.log(jax.random.uniform(ks[11], (DEPTH, SSM_HEADS), F32, 1.0, 16.0)),
        'ssm_d': 1.0 + nrm(ks[12], (DEPTH, SSM_HEADS), 0.01),
        'ssm_norm_g': 1.0 + nrm(ks[13], (DEPTH, SSM_D_INNER), 0.01),
        'diff_lq1': nrm(ks[14], (DEPTH, DIFF_HEAD_DIM), 0.1),
        'diff_lk1': nrm(ks[15], (DEPTH, DIFF_HEAD_DIM), 0.1),
        'diff_lq2': nrm(ks[16], (DEPTH, DIFF_HEAD_DIM), 0.1),
        'diff_lk2': nrm(ks[17], (DEPTH, DIFF_HEAD_DIM), 0.1),
        'diff_norm_g': 1.0 + nrm(ks[18], (DEPTH, DIFF_V_DIM), 0.01),
        'w_br_gla': nrm(ks[19], (DEPTH, GLA_VAL_DIM, D_MODEL), GLA_VAL_DIM ** -0.5),
        'w_br_ssm': nrm(ks[20], (DEPTH, SSM_D_INNER, D_MODEL), SSM_D_INNER ** -0.5),
        'w_br_diff': nrm(ks[21], (DEPTH, DIFF_V_WIDTH, D_MODEL), DIFF_V_WIDTH ** -0.5),
        'w_out': nrm(ks[22], (DEPTH, D_MODEL, D_MODEL), D_MODEL ** -0.5),
        'norm_mlp_g': 1.0 + nrm(ks[23], (DEPTH, D_MODEL), 0.01),
        'w_mlp_up': nrm(ks[24], (DEPTH, D_MODEL, D_FF), D_MODEL ** -0.5),
        'w_mlp_down': nrm(ks[25], (DEPTH, D_FF, D_MODEL), D_FF ** -0.5),
        'norm_final_g': 1.0 + nrm(ks[26], (D_MODEL,), 0.01),
    }


def reference(x, positions, norm_mix_g, w_in, b_gate, gla_w_gk2, gla_b_gk, gla_norm_g,
              ssm_conv_w, ssm_conv_b, ssm_dt_bias, ssm_a_log, ssm_d, ssm_norm_g,
              diff_lq1, diff_lk1, diff_lq2, diff_lk2, diff_norm_g,
              w_br_gla, w_br_ssm, w_br_diff, w_out, norm_mlp_g, w_mlp_up, w_mlp_down,
              norm_final_g):
    bsz, s_len, _ = x.shape
    pts = _split_points()
    for l in range(DEPTH):
        h = rmsnorm(x, norm_mix_g[l])
        proj = h @ w_in[l]
        (a_q, a_k, a_v, a_gk, a_g, b_z, b_xbc, b_dt,
         c_q, c_k, c_v, gate_logits) = jnp.split(proj, pts, axis=-1)
        y_gla = gla_mixer(a_q, a_k, a_v, a_gk, a_g, gla_w_gk2[l], gla_b_gk[l], gla_norm_g[l])
        y_ssm = ssd_mixer(b_z, b_xbc, b_dt, ssm_conv_w[l], ssm_conv_b[l], ssm_dt_bias[l],
                          ssm_a_log[l], ssm_d[l], ssm_norm_g[l])
        lambda_init = 0.8 - 0.6 * math.exp(-0.3 * l)
        y_diff = diff_mixer(c_q, c_k, c_v, positions, diff_lq1[l], diff_lk1[l], diff_lq2[l],
                            diff_lk2[l], diff_norm_g[l], lambda_init)
        gates = jax.nn.sigmoid((gate_logits + b_gate[l]).astype(F32))
        gates = gates.reshape(bsz, s_len, N_BRANCHES, D_MODEL).astype(x.dtype)
        mixed = (gates[:, :, 0] * (y_gla @ w_br_gla[l])
                 + gates[:, :, 1] * (y_ssm @ w_br_ssm[l])
                 + gates[:, :, 2] * (y_diff.astype(x.dtype) @ w_br_diff[l]))
        x = x + mixed @ w_out[l]
        h = rmsnorm(x, norm_mlp_g[l])
        x = x + jnp.square(jax.nn.relu(h @ w_mlp_up[l])) @ w_mlp_down[l]
    return rmsnorm(x, norm_final_g)
```

```python
import functools
import math

import jax
import jax.numpy as jnp
import numpy as np
from jax import lax
from jax.experimental import pallas as pl
from jax.experimental.pallas import tpu as pltpu

F32 = jnp.float32
BF16 = jnp.bfloat16

D_MODEL = 1024
DEPTH = 2
CHUNK = 64
ROPE_THETA = 500000.0
EPS = 1e-6

GLA_HEADS = 4
GLA_KEY_DIM = D_MODEL // 2
GLA_VAL_DIM = D_MODEL
GLA_DK = GLA_KEY_DIM // GLA_HEADS
GLA_DV = GLA_VAL_DIM // GLA_HEADS
GLA_LOW_RANK = 16
GLA_GATE_NORMALIZER = 16.0

SSM_D_INNER = 2 * D_MODEL
SSM_HEAD_DIM = 64
SSM_HEADS = SSM_D_INNER // SSM_HEAD_DIM
SSM_GROUPS = 4
SSM_HEADS_PER_GROUP = SSM_HEADS // SSM_GROUPS
SSM_STATE = 128
SSM_CONV_W = 4
SSM_BC_WIDTH = SSM_GROUPS * SSM_STATE
SSM_CONV_DIM = SSM_D_INNER + 2 * SSM_BC_WIDTH
SSM_GROUP_WIDTH = SSM_D_INNER // SSM_GROUPS

DIFF_HEADS = 8
DIFF_HEAD_DIM = 64
DIFF_V_DIM = 2 * DIFF_HEAD_DIM
DIFF_QK_WIDTH = DIFF_HEADS * 2 * DIFF_HEAD_DIM
DIFF_V_WIDTH = DIFF_HEADS * DIFF_V_DIM
ROT_DIM = DIFF_HEAD_DIM // 4
DIFF_SUBLN_EPS = 1e-5

N_BRANCHES = 3
D_FF = 4 * D_MODEL

LANES = 128
MASK_VALUE = -1e30

COL_XBC = 0
COL_GATE = COL_XBC + SSM_CONV_DIM
COL_GLA_V = COL_GATE + N_BRANCHES * D_MODEL
COL_GLA_G = COL_GLA_V + GLA_VAL_DIM
COL_DIFF_Q = COL_GLA_G + GLA_VAL_DIM
COL_DIFF_K = COL_DIFF_Q + DIFF_QK_WIDTH
COL_DIFF_V = COL_DIFF_K + DIFF_QK_WIDTH
COL_GLA_Q = COL_DIFF_V + DIFF_V_WIDTH
COL_GLA_K = COL_GLA_Q + GLA_KEY_DIM
COL_SSM_Z = COL_GLA_K + GLA_KEY_DIM
SLAB_COLS = COL_SSM_Z + SSM_D_INNER
SMALL_COLS = 2 * LANES

VMEM_LIMIT = 56 * 1024 * 1024


def _cparams(sem):
    return pltpu.CompilerParams(dimension_semantics=sem, vmem_limit_bytes=VMEM_LIMIT)


def _dot(a, b):
    return jnp.dot(a, b, preferred_element_type=F32)


def _dot_nt(a, b):
    return lax.dot_general(a, b, (((1,), (1,)), ((), ())), preferred_element_type=F32)


def _split3(x):
    hi = x.astype(BF16)
    r1 = x - hi.astype(F32)
    mid = r1.astype(BF16)
    lo = (r1 - mid.astype(F32)).astype(BF16)
    return hi, mid, lo


def _dot_sel_lhs(sel, x):
    hi, mid, lo = _split3(x)
    return _dot(sel, hi) + _dot(sel, mid) + _dot(sel, lo)


def _dot_sel_rhs(x, sel):
    hi, mid, lo = _split3(x)
    return _dot(hi, sel) + _dot(mid, sel) + _dot(lo, sel)


def _dot_nt_sel_lhs(sel, x):
    hi, mid, lo = _split3(x)
    return _dot_nt(sel, hi) + _dot_nt(sel, mid) + _dot_nt(sel, lo)


def _dot_f32(a, b):
    ah, am, _ = _split3(a)
    bh, bm, _ = _split3(b)
    return _dot(ah, bh) + _dot(ah, bm) + _dot(am, bh)


def _silu(x):
    return x / (1.0 + jnp.exp(-x))


def _log1p_exp_neg_abs(x):
    return jnp.log(1.0 + jnp.exp(-jnp.abs(x)))


def _log_sigmoid(x):
    return jnp.minimum(x, 0.0) - _log1p_exp_neg_abs(x)


def _softplus(x):
    return jnp.maximum(x, 0.0) + _log1p_exp_neg_abs(x)


def _iota(shape, dim):
    return lax.broadcasted_iota(jnp.int32, shape, dim)


def _inproj_kernel(x_ref, g_ref, w_ref, ws_ref, o_ref, os_ref, h_ref):
    @pl.when(pl.program_id(1) == 0)
    def _():
        x = x_ref[...]
        ms = jnp.mean(x * x, axis=-1, keepdims=True)
        h = (x * lax.rsqrt(ms + EPS) * g_ref[...]).astype(BF16)
        h_ref[...] = h
        os_ref[...] = _dot(h, ws_ref[...])

    o_ref[...] = _dot(h_ref[...], w_ref[...]).astype(o_ref.dtype)


def _inproj(x2, g, w_slab, w_small, tm=1024, tn=1024):
    m = x2.shape[0]
    return pl.pallas_call(
        _inproj_kernel,
        grid=(m // tm, SLAB_COLS // tn),
        in_specs=[
            pl.BlockSpec((tm, D_MODEL), lambda i, j: (i, 0)),
            pl.BlockSpec((1, D_MODEL), lambda i, j: (0, 0)),
            pl.BlockSpec((D_MODEL, tn), lambda i, j: (0, j)),
            pl.BlockSpec((D_MODEL, SMALL_COLS), lambda i, j: (0, 0)),
        ],
        out_specs=[
            pl.BlockSpec((tm, tn), lambda i, j: (i, j)),
            pl.BlockSpec((tm, SMALL_COLS), lambda i, j: (i, 0)),
        ],
        out_shape=[
            jax.ShapeDtypeStruct((m, SLAB_COLS), BF16),
            jax.ShapeDtypeStruct((m, SMALL_COLS), F32),
        ],
        scratch_shapes=[pltpu.VMEM((tm, D_MODEL), BF16)],
        compiler_params=_cparams(("parallel", "arbitrary")),
        name="inproj",
    )(x2, g, w_slab, w_small)


def _level_anchor(b, h):
    c = CHUNK
    if 2 * h >= 8:
        parts = [jnp.broadcast_to(b[s + h - 1:s + h, :], (2 * h, b.shape[1]))
                 for s in range(0, c, 2 * h)]
        return parts[0] if len(parts) == 1 else jnp.concatenate(parts, axis=0)
    pos = _iota(b.shape, 0) % (2 * h)
    if h == 1:
        return jnp.where(pos == 1, pltpu.roll(b, 1, 0), b)
    return jnp.where(pos == 0, pltpu.roll(b, c - 1, 0),
                     jnp.where(pos == 1, b,
                               jnp.where(pos == 2, pltpu.roll(b, 1, 0), pltpu.roll(b, 2, 0))))


def _gla_kernel(q_ref, k_ref, v_ref, g_ref, sm_ref, wgk_ref, bgk_ref, ng_ref, o_ref, st_ref):
    c = CHUNK

    @pl.when(pl.program_id(1) == 0)
    def _():
        st_ref[...] = jnp.zeros_like(st_ref)

    gk = _log_sigmoid(_dot_f32(sm_ref[...], wgk_ref[...]) + bgk_ref[...]) / GLA_GATE_NORMALIZER
    row = _iota((c, c), 0)
    col = _iota((c, c), 1)
    tril = jnp.where(col <= row, 1.0, 0.0).astype(BF16)
    b_all = _dot_sel_lhs(tril, gk)
    eye = row == col

    for hd in range(GLA_HEADS):
        ksl = slice(hd * GLA_DK, (hd + 1) * GLA_DK)
        vsl = slice(hd * GLA_DV, (hd + 1) * GLA_DV)
        q = q_ref[:, ksl].astype(F32) * (GLA_DK ** -0.5)
        k = k_ref[:, ksl].astype(F32)
        v = v_ref[:, vsl]
        b = b_all[:, ksl]
        b_last = b[c - 1:c, :]
        st = st_ref[hd]

        o = _dot_nt((q * jnp.exp(b)).astype(BF16), st.astype(BF16))

        scores = jnp.where(eye, _dot_nt(q.astype(BF16), k.astype(BF16)), 0.0)
        h = c // 2
        while h >= 1:
            anchor = _level_anchor(b, h)
            second = (_iota(b.shape, 0) % (2 * h)) >= h
            decay = jnp.exp(jnp.where(second, b - anchor, anchor - b))
            qt = jnp.where(second, q * decay, 0.0).astype(BF16)
            kt = jnp.where(second, 0.0, k * decay).astype(BF16)
            p = _dot_nt(qt, kt)
            if 2 * h < c:
                p = jnp.where((row // (2 * h)) == (col // (2 * h)), p, 0.0)
            scores = scores + p
            h //= 2
        o = o + _dot(scores.astype(BF16), v)

        kd = (k * jnp.exp(b_last - b)).astype(BF16)
        vt = v.astype(F32).T.astype(BF16)
        st_ref[hd] = st * jnp.exp(b_last) + _dot(vt, kd)

        g = g_ref[:, vsl].astype(F32)
        ms = jnp.mean(o * o, axis=-1, keepdims=True)
        o_ref[:, vsl] = (o * lax.rsqrt(ms + EPS) * ng_ref[...] * _silu(g)).astype(o_ref.dtype)


def _gla(slab, small, wgk_pad, b_gk, norm_g, bsz, s_len):
    nc = s_len // CHUNK
    rb = lambda b, i: b * nc + i
    return pl.pallas_call(
        _gla_kernel,
        grid=(bsz, nc),
        in_specs=[
            pl.BlockSpec((CHUNK, GLA_KEY_DIM), lambda b, i: (rb(b, i), COL_GLA_Q // GLA_KEY_DIM)),
            pl.BlockSpec((CHUNK, GLA_KEY_DIM), lambda b, i: (rb(b, i), COL_GLA_K // GLA_KEY_DIM)),
            pl.BlockSpec((CHUNK, GLA_VAL_DIM), lambda b, i: (rb(b, i), COL_GLA_V // GLA_VAL_DIM)),
            pl.BlockSpec((CHUNK, GLA_VAL_DIM), lambda b, i: (rb(b, i), COL_GLA_G // GLA_VAL_DIM)),
            pl.BlockSpec((CHUNK, LANES), lambda b, i: (rb(b, i), 0)),
            pl.BlockSpec((LANES, GLA_KEY_DIM), lambda b, i: (0, 0)),
            pl.BlockSpec((1, GLA_KEY_DIM), lambda b, i: (0, 0)),
            pl.BlockSpec((1, GLA_DV), lambda b, i: (0, 0)),
        ],
        out_specs=pl.BlockSpec((CHUNK, GLA_VAL_DIM), lambda b, i: (rb(b, i), 0)),
        out_shape=jax.ShapeDtypeStruct((bsz * s_len, GLA_VAL_DIM), BF16),
        scratch_shapes=[pltpu.VMEM((GLA_HEADS, GLA_DV, GLA_DK), F32)],
        compiler_params=_cparams(("parallel", "arbitrary")),
        name="gla",
    )(slab, slab, slab, slab, small, wgk_pad, b_gk, norm_g)


def _ssd_kernel(xbc_ref, z_ref, sm_ref, cw_ref, cb_ref, dtb_ref, alog_ref, dexp_ref, ng_ref,
                expand_ref, pairsel_ref, o_ref, xpad_ref, st_ref, yd_ref):
    c = CHUNK
    n = SSM_STATE
    gw = SSM_GROUP_WIDTH
    pad = 8

    @pl.when(pl.program_id(1) == 0)
    def _():
        st_ref[...] = jnp.zeros_like(st_ref)
        xpad_ref[0:pad, :] = jnp.zeros((pad, SSM_CONV_DIM), F32)

    xpad_ref[pad:pad + c, :] = xbc_ref[...].astype(F32)
    conv = cb_ref[...]
    for i in range(SSM_CONV_W):
        off = pad - (SSM_CONV_W - 1) + i
        conv = conv + xpad_ref[off:off + c, :] * cw_ref[i:i + 1, :]
    xpad_ref[0:pad, :] = xpad_ref[c:c + pad, :]
    xc = _silu(conv)
    xs = xc[:, :SSM_D_INNER]

    dt = _softplus(sm_ref[...] + dtb_ref[...])
    da = dt * (-jnp.exp(alog_ref[...]))
    row = _iota((c, c), 0)
    col = _iota((c, c), 1)
    tril = jnp.where(col <= row, 1.0, 0.0).astype(BF16)
    a_cum = _dot_sel_lhs(tril, da)

    both = _dot_sel_rhs(jnp.concatenate([dt, a_cum], axis=0), expand_ref[...])
    dt_e = both[:c]
    a_e = both[c:]
    a_last_e = a_e[c - 1:c, :]
    xdt = xs * dt_e
    decay_out = jnp.exp(a_e)
    xdtd = (xdt * jnp.exp(a_last_e - a_e)).astype(BF16)
    chunk_decay = jnp.exp(a_last_e)

    lane = _iota((c, LANES), 1)
    z0 = jnp.concatenate([jnp.where(lane % 2 == 0, a_cum, 0.0),
                          jnp.where(lane % 2 == 1, a_cum, 0.0)], axis=0)
    pairs = _dot_nt_sel_lhs(pairsel_ref[...], z0)

    lane2 = _iota((c, 2 * c), 1)
    row2 = _iota((c, 2 * c), 0)
    causal2 = (lane2 % c) <= row2
    bd_row = _iota((2 * c, 2 * c), 0)
    bd_col = _iota((2 * c, 2 * c), 1)
    bd_mask = (bd_row // c) == (bd_col // c)

    for g in range(SSM_GROUPS):
        bg = xc[:, SSM_D_INNER + g * n:SSM_D_INNER + (g + 1) * n]
        cg = xc[:, SSM_D_INNER + SSM_BC_WIDTH + g * n:SSM_D_INNER + SSM_BC_WIDTH + (g + 1) * n]
        cg16 = cg.astype(BF16)
        bg16 = bg.astype(BF16)
        gsl = slice(g * gw, (g + 1) * gw)
        st = st_ref[g]

        y_off = _dot(cg16, st.astype(BF16)) * decay_out[:, gsl]
        st_ref[g] = st * chunk_decay[:, gsl] + _dot(bg.T.astype(BF16), xdtd[:, gsl])

        cb2 = _dot_nt(cg16, jnp.concatenate([bg16, bg16], axis=0))
        for pr in range(SSM_HEADS_PER_GROUP // 2):
            pidx = g * (SSM_HEADS_PER_GROUP // 2) + pr
            lsl = slice(pidx * 2 * c, (pidx + 1) * 2 * c)
            seg = a_e[:, lsl] - pairs[pidx:pidx + 1, :]
            lmat = jnp.exp(jnp.where(causal2, seg, MASK_VALUE))
            w = (cb2 * lmat).astype(BF16)
            xp = xdt[:, lsl]
            bd = jnp.where(bd_mask, jnp.concatenate([xp, xp], axis=0), 0.0).astype(BF16)
            yd_ref[:, lsl] = _dot(w, bd)

        y = yd_ref[:, gsl] + y_off + xs[:, gsl] * dexp_ref[:, gsl]
        y = y * _silu(z_ref[:, gsl].astype(F32))
        ms = jnp.mean(y * y, axis=-1, keepdims=True)
        o_ref[:, gsl] = (y * lax.rsqrt(ms + EPS) * ng_ref[:, gsl]).astype(o_ref.dtype)


def _ssd(slab, small, conv_w, conv_b, dtb_pad, alog_pad, d_exp, norm_g, expand, pairsel, bsz, s_len):
    nc = s_len // CHUNK
    rb = lambda b, i: b * nc + i
    const = lambda b, i: (0, 0)
    return pl.pallas_call(
        _ssd_kernel,
        grid=(bsz, nc),
        in_specs=[
            pl.BlockSpec((CHUNK, SSM_CONV_DIM), lambda b, i: (rb(b, i), COL_XBC // SSM_CONV_DIM)),
            pl.BlockSpec((CHUNK, SSM_D_INNER), lambda b, i: (rb(b, i), COL_SSM_Z // SSM_D_INNER)),
            pl.BlockSpec((CHUNK, LANES), lambda b, i: (rb(b, i), 1)),
            pl.BlockSpec((SSM_CONV_W, SSM_CONV_DIM), const),
            pl.BlockSpec((1, SSM_CONV_DIM), const),
            pl.BlockSpec((1, LANES), const),
            pl.BlockSpec((1, LANES), const),
            pl.BlockSpec((1, SSM_D_INNER), const),
            pl.BlockSpec((1, SSM_D_INNER), const),
            pl.BlockSpec((LANES, SSM_D_INNER), const),
            pl.BlockSpec((SSM_HEADS // 2, LANES), const),
        ],
        out_specs=pl.BlockSpec((CHUNK, SSM_D_INNER), lambda b, i: (rb(b, i), 0)),
        out_shape=jax.ShapeDtypeStruct((bsz * s_len, SSM_D_INNER), BF16),
        scratch_shapes=[
            pltpu.VMEM((CHUNK + 8, SSM_CONV_DIM), F32),
            pltpu.VMEM((SSM_GROUPS, SSM_STATE, SSM_GROUP_WIDTH), F32),
            pltpu.VMEM((CHUNK, SSM_D_INNER), F32),
        ],
        compiler_params=_cparams(("parallel", "arbitrary")),
        name="ssd",
    )(slab, slab, small, conv_w, conv_b, dtb_pad, alog_pad, d_exp, norm_g, expand, pairsel)


def _rope_table_kernel(pos_ref, freq_ref, c_ref, s1_ref, s2_ref):
    ang = pos_ref[...].astype(F32) * freq_ref[...]
    d = _iota(ang.shape, 1) % DIFF_HEAD_DIM
    half = ROT_DIM // 2
    cosv = jnp.cos(ang)
    sinv = jnp.sin(ang)
    c_ref[...] = jnp.where(d < ROT_DIM, cosv, 1.0)
    s1_ref[...] = jnp.where(d < half, -sinv, 0.0)
    s2_ref[...] = jnp.where((d >= half) & (d < ROT_DIM), sinv, 0.0)


def _rope_tables(pos_col, freq_row, tr=512):
    m = pos_col.shape[0]
    spec = pl.BlockSpec((tr, LANES), lambda i: (i, 0))
    shp = jax.ShapeDtypeStruct((m, LANES), F32)
    return pl.pallas_call(
        _rope_table_kernel,
        grid=(m // tr,),
        in_specs=[pl.BlockSpec((tr, 1), lambda i: (i, 0)), pl.BlockSpec((1, LANES), lambda i: (0, 0))],
        out_specs=[spec, spec, spec],
        out_shape=[shp, shp, shp],
        compiler_params=_cparams(("parallel",)),
        name="rope_tables",
    )(pos_col, freq_row)


def _attn_prep_kernel(q_ref, k_ref, v_ref, c_ref, s1_ref, s2_ref, qo_ref, ko_ref, vt_ref):
    half = ROT_DIM // 2
    cosf = c_ref[...]
    s1 = s1_ref[...]
    s2 = s2_ref[...]

    def rope(t):
        return t * cosf + pltpu.roll(t, LANES - half, 1) * s1 + pltpu.roll(t, half, 1) * s2

    for hd in range(DIFF_HEADS):
        sl = slice(hd * LANES, (hd + 1) * LANES)
        q = q_ref[:, sl].astype(F32)
        k = k_ref[:, sl].astype(F32)
        qo_ref[:, sl] = (rope(q) * (DIFF_HEAD_DIM ** -0.5)).astype(BF16)
        ko_ref[:, sl] = rope(k).astype(BF16)
        vt_ref[0, hd, 0] = v_ref[:, sl].astype(F32).T.astype(BF16)


def _attn_prep(slab, cosf, s1, s2, bsz, s_len, tr):
    m = bsz * s_len
    nb = s_len // tr
    w = DIFF_QK_WIDTH
    row = pl.BlockSpec((tr, w), lambda i: (i, 0))
    tab = pl.BlockSpec((tr, LANES), lambda i: (i, 0))
    return pl.pallas_call(
        _attn_prep_kernel,
        grid=(m // tr,),
        in_specs=[
            pl.BlockSpec((tr, w), lambda i: (i, COL_DIFF_Q // w)),
            pl.BlockSpec((tr, w), lambda i: (i, COL_DIFF_K // w)),
            pl.BlockSpec((tr, w), lambda i: (i, COL_DIFF_V // w)),
            tab, tab, tab,
        ],
        out_specs=[
            row, row,
            pl.BlockSpec((1, DIFF_HEADS, 1, DIFF_V_DIM, tr), lambda i: (i // nb, 0, i % nb, 0, 0)),
        ],
        out_shape=[
            jax.ShapeDtypeStruct((m, w), BF16),
            jax.ShapeDtypeStruct((m, w), BF16),
            jax.ShapeDtypeStruct((bsz, DIFF_HEADS, nb, DIFF_V_DIM, tr), BF16),
        ],
        compiler_params=_cparams(("parallel",)),
        name="attn_prep",
    )(slab, slab, slab, cosf, s1, s2)


def _flash_kernel(q_ref, k_ref, vt_ref, lq1_ref, lk1_ref, lq2_ref, lk2_ref, ng_ref, o_ref,
                  *, tq, lambda_init):
    qi = pl.program_id(2)
    q = q_ref[...]
    lane = _iota(q.shape, 1)
    zero = jnp.zeros_like(q)
    qs = jnp.concatenate([jnp.where(lane < DIFF_HEAD_DIM, q, zero),
                          jnp.where(lane >= DIFF_HEAD_DIM, q, zero)], axis=0)

    def step(j, carry, masked):
        m, l, acc = carry
        kj = k_ref[pl.ds(pl.multiple_of(j * tq, tq), tq), :]
        s = _dot_nt(kj, qs)
        if masked:
            kc = _iota(s.shape, 0) // CHUNK
            qc = (_iota(s.shape, 1) % tq) // CHUNK
            s = jnp.where(kc <= qc, s, MASK_VALUE)
        m_new = jnp.maximum(m, jnp.max(s, axis=0, keepdims=True))
        p = jnp.exp(s - m_new)
        alpha = jnp.exp(m - m_new)
        l = alpha * l + jnp.sum(p, axis=0, keepdims=True)
        acc = acc * alpha + _dot(vt_ref[0, 0, j], p.astype(BF16))
        return m_new, l, acc

    init = (jnp.full((1, 2 * tq), MASK_VALUE, F32), jnp.zeros((1, 2 * tq), F32),
            jnp.zeros((DIFF_V_DIM, 2 * tq), F32))
    carry = lax.fori_loop(0, qi, lambda j, cr: step(j, cr, False), init)
    m, l, acc = step(qi, carry, True)

    lam = (jnp.exp(jnp.sum(lq1_ref[...] * lk1_ref[...], keepdims=True))
           - jnp.exp(jnp.sum(lq2_ref[...] * lk2_ref[...], keepdims=True)) + lambda_init)
    on = acc / l
    o = on[:, :tq] - lam * on[:, tq:]
    ms = jnp.mean(o * o, axis=0, keepdims=True)
    o = o * lax.rsqrt(ms + DIFF_SUBLN_EPS)
    o_ref[...] = (o.T * ng_ref[...] * (1.0 - lambda_init)).astype(o_ref.dtype)


def _flash(q_r, k_r, v_t, lq1, lk1, lq2, lk2, norm_g, bsz, s_len, tq, lambda_init):
    nq = s_len // tq
    vec = pl.BlockSpec((1, DIFF_HEAD_DIM), lambda b, h, i: (0, 0))
    return pl.pallas_call(
        functools.partial(_flash_kernel, tq=tq, lambda_init=lambda_init),
        grid=(bsz, DIFF_HEADS, nq),
        in_specs=[
            pl.BlockSpec((tq, LANES), lambda b, h, i: (b * nq + i, h)),
            pl.BlockSpec((s_len, LANES), lambda b, h, i: (b, h)),
            pl.BlockSpec((1, 1, nq, DIFF_V_DIM, tq), lambda b, h, i: (b, h, 0, 0, 0)),
            vec, vec, vec, vec,
            pl.BlockSpec((1, DIFF_V_DIM), lambda b, h, i: (0, 0)),
        ],
        out_specs=pl.BlockSpec((tq, DIFF_V_DIM), lambda b, h, i: (b * nq + i, h)),
        out_shape=jax.ShapeDtypeStruct((bsz * s_len, DIFF_V_WIDTH), BF16),
        compiler_params=_cparams(("parallel", "parallel", "arbitrary")),
        name="flash",
    )(q_r, k_r, v_t, lq1, lk1, lq2, lk2, norm_g)


def _merge_kernel(x_ref, yg_ref, ys_ref, yd_ref, g0_ref, g1_ref, g2_ref, bg_ref,
                  wg_ref, ws_ref, wd_ref, wo_ref, o_ref):
    def gate(g_ref, i):
        return 1.0 / (1.0 + jnp.exp(-(g_ref[...].astype(F32) + bg_ref[i:i + 1, :])))

    mixed = gate(g0_ref, 0) * _dot(yg_ref[...], wg_ref[...])
    mixed = mixed + gate(g1_ref, 1) * _dot(ys_ref[...], ws_ref[...])
    mixed = mixed + gate(g2_ref, 2) * _dot(yd_ref[...], wd_ref[...])
    o_ref[...] = x_ref[...] + _dot(mixed.astype(BF16), wo_ref[...])


def _merge(x2, y_gla, y_ssm, y_diff, slab, b_gate, w_g, w_s, w_d, w_o, tm=512):
    m = x2.shape[0]
    gcol = COL_GATE // D_MODEL
    row = lambda w: pl.BlockSpec((tm, w), lambda i: (i, 0))
    const = lambda r, c: pl.BlockSpec((r, c), lambda i: (0, 0))
    return pl.pallas_call(
        _merge_kernel,
        grid=(m // tm,),
        in_specs=[
            row(D_MODEL), row(GLA_VAL_DIM), row(SSM_D_INNER), row(DIFF_V_WIDTH),
            pl.BlockSpec((tm, D_MODEL), lambda i: (i, gcol)),
            pl.BlockSpec((tm, D_MODEL), lambda i: (i, gcol + 1)),
            pl.BlockSpec((tm, D_MODEL), lambda i: (i, gcol + 2)),
            const(N_BRANCHES, D_MODEL),
            const(GLA_VAL_DIM, D_MODEL), const(SSM_D_INNER, D_MODEL), const(DIFF_V_WIDTH, D_MODEL),
            const(D_MODEL, D_MODEL),
        ],
        out_specs=row(D_MODEL),
        out_shape=jax.ShapeDtypeStruct((m, D_MODEL), F32),
        compiler_params=_cparams(("parallel",)),
        name="merge",
    )(x2, y_gla, y_ssm, y_diff, slab, slab, slab, b_gate, w_g, w_s, w_d, w_o)


def _mlp_kernel(x_ref, g_ref, wu_ref, wd_ref, gf_ref, o_ref, *, final_norm, fc):
    x = x_ref[...]
    ms = jnp.mean(x * x, axis=-1, keepdims=True)
    h = (x * lax.rsqrt(ms + EPS) * g_ref[...]).astype(BF16)
    acc = x
    for c0 in range(0, D_FF, fc):
        u = jnp.maximum(_dot(h, wu_ref[:, c0:c0 + fc]), 0.0)
        acc = acc + _dot((u * u).astype(BF16), wd_ref[c0:c0 + fc, :])
    if final_norm:
        ms = jnp.mean(acc * acc, axis=-1, keepdims=True)
        acc = acc * lax.rsqrt(ms + EPS) * gf_ref[...]
    o_ref[...] = acc


def _mlp(x2, g, w_up, w_down, g_final, final_norm, tm=512, fc=1024):
    m = x2.shape[0]
    row = pl.BlockSpec((tm, D_MODEL), lambda i: (i, 0))
    const = lambda r, c: pl.BlockSpec((r, c), lambda i: (0, 0))
    return pl.pallas_call(
        functools.partial(_mlp_kernel, final_norm=final_norm, fc=fc),
        grid=(m // tm,),
        in_specs=[row, const(1, D_MODEL), const(D_MODEL, D_FF), const(D_FF, D_MODEL), const(1, D_MODEL)],
        out_specs=row,
        out_shape=jax.ShapeDtypeStruct((m, D_MODEL), F32),
        compiler_params=_cparams(("parallel",)),
        name="mlp",
    )(x2, g, w_up, w_down, g_final)


def _slab_weights(w_in_l):
    sizes = (GLA_KEY_DIM, GLA_KEY_DIM, GLA_VAL_DIM, GLA_LOW_RANK, GLA_VAL_DIM,
             SSM_D_INNER, SSM_CONV_DIM, SSM_HEADS,
             DIFF_QK_WIDTH, DIFF_QK_WIDTH, DIFF_V_WIDTH, N_BRANCHES * D_MODEL)
    offs = np.concatenate([[0], np.cumsum(sizes)])
    (a_q, a_k, a_v, a_gk, a_g, b_z, b_xbc, b_dt, c_q, c_k, c_v, gate) = [
        w_in_l[:, int(offs[i]):int(offs[i + 1])] for i in range(len(sizes))]
    slab = jnp.concatenate([b_xbc, gate, a_v, a_g, c_q, c_k, c_v, a_q, a_k, b_z], axis=1).astype(BF16)
    zeros = lambda n: jnp.zeros((D_MODEL, n), w_in_l.dtype)
    small = jnp.concatenate([a_gk, zeros(LANES - GLA_LOW_RANK), b_dt, zeros(LANES - SSM_HEADS)],
                            axis=1).astype(BF16)
    return slab, small


def _pad_lanes(v, fill=0.0):
    v = v.reshape(1, -1).astype(F32)
    return jnp.pad(v, ((0, 0), (0, LANES - v.shape[1])), constant_values=fill)


def _forward(x, positions, norm_mix_g, w_in, b_gate, gla_w_gk2, gla_b_gk, gla_norm_g,
             ssm_conv_w, ssm_conv_b, ssm_dt_bias, ssm_a_log, ssm_d, ssm_norm_g,
             diff_lq1, diff_lk1, diff_lq2, diff_lk2, diff_norm_g,
             w_br_gla, w_br_ssm, w_br_diff, w_out, norm_mlp_g, w_mlp_up, w_mlp_down,
             norm_final_g, tq):
    bsz, s_len, _ = x.shape
    m = bsz * s_len
    depth = w_in.shape[0]
    x2 = x.reshape(m, D_MODEL).astype(F32)

    head_of_lane = np.arange(SSM_D_INNER) // SSM_HEAD_DIM
    expand = jnp.asarray((np.arange(LANES)[:, None] == head_of_lane[None, :]), BF16)
    pairsel = jnp.asarray((np.arange(LANES)[None, :] // 2) == np.arange(SSM_HEADS // 2)[:, None], BF16)
    pairsel = pairsel * jnp.asarray(np.arange(LANES)[None, :] < SSM_HEADS, BF16)
    d = np.arange(LANES) % DIFF_HEAD_DIM
    inv_freq = ROPE_THETA ** (-(2.0 * (d % (ROT_DIM // 2))) / ROT_DIM)
    freq_row = jnp.asarray(np.where(d < ROT_DIM, inv_freq, 0.0).reshape(1, LANES), F32)

    cosf, s1, s2 = _rope_tables(positions.reshape(m, 1), freq_row, tr=min(512, s_len))

    for l in range(depth):
        w_slab, w_small = _slab_weights(w_in[l])
        slab, small = _inproj(x2, norm_mix_g[l].reshape(1, -1), w_slab, w_small,
                              tm=min(1024, m))

        wgk_pad = jnp.pad(gla_w_gk2[l].astype(F32), ((0, LANES - GLA_LOW_RANK), (0, 0)))
        y_gla = _gla(slab, small, wgk_pad, gla_b_gk[l].reshape(1, -1), gla_norm_g[l].reshape(1, -1),
                     bsz, s_len)

        d_exp = jnp.repeat(ssm_d[l].astype(F32), SSM_HEAD_DIM).reshape(1, -1)
        y_ssm = _ssd(slab, small, ssm_conv_w[l], ssm_conv_b[l].reshape(1, -1),
                     _pad_lanes(ssm_dt_bias[l]), _pad_lanes(ssm_a_log[l]), d_exp,
                     ssm_norm_g[l].reshape(1, -1), expand, pairsel, bsz, s_len)

        lambda_init = 0.8 - 0.6 * math.exp(-0.3 * l)
        q_r, k_r, v_t = _attn_prep(slab, cosf, s1, s2, bsz, s_len, tq)
        y_diff = _flash(q_r, k_r, v_t, diff_lq1[l].reshape(1, -1), diff_lk1[l].reshape(1, -1),
                        diff_lq2[l].reshape(1, -1), diff_lk2[l].reshape(1, -1),
                        diff_norm_g[l].reshape(1, -1), bsz, s_len, tq, lambda_init)

        x2 = _merge(x2, y_gla, y_ssm, y_diff, slab, b_gate[l].reshape(N_BRANCHES, D_MODEL),
                    w_br_gla[l].astype(BF16), w_br_ssm[l].astype(BF16), w_br_diff[l].astype(BF16),
                    w_out[l].astype(BF16), tm=min(512, m))
        x2 = _mlp(x2, norm_mlp_g[l].reshape(1, -1), w_mlp_up[l].astype(BF16),
                  w_mlp_down[l].astype(BF16), norm_final_g.reshape(1, -1),
                  final_norm=(l == depth - 1), tm=min(512, m))
    return x2.reshape(bsz, s_len, D_MODEL)


def kernel(x, positions, norm_mix_g, w_in, b_gate, gla_w_gk2, gla_b_gk, gla_norm_g, ssm_conv_w,
           ssm_conv_b, ssm_dt_bias, ssm_a_log, ssm_d, ssm_norm_g, diff_lq1, diff_lk1, diff_lq2,
           diff_lk2, diff_norm_g, w_br_gla, w_br_ssm, w_br_diff, w_out, norm_mlp_g, w_mlp_up,
           w_mlp_down, norm_final_g):
    return _forward(x, positions, norm_mix_g, w_in, b_gate, gla_w_gk2, gla_b_gk, gla_norm_g,
                    ssm_conv_w, ssm_conv_b, ssm_dt_bias, ssm_a_log, ssm_d, ssm_norm_g,
                    diff_lq1, diff_lk1, diff_lq2, diff_lk2, diff_norm_g,
                    w_br_gla, w_br_ssm, w_br_diff, w_out, norm_mlp_g, w_mlp_up, w_mlp_down,
                    norm_final_g, tq=min(512, x.shape[1]))
```

```python
import functools
import math

import jax
import jax.numpy as jnp
import numpy as np
from jax import lax
from jax.experimental import pallas as pl
from jax.experimental.pallas import tpu as pltpu

F32 = jnp.float32
BF16 = jnp.bfloat16

D_MODEL = 1024
DEPTH = 2
CHUNK = 64
ROPE_THETA = 500000.0
EPS = 1e-6

GLA_HEADS = 4
GLA_KEY_DIM = D_MODEL // 2
GLA_VAL_DIM = D_MODEL
GLA_DK = GLA_KEY_DIM // GLA_HEADS
GLA_DV = GLA_VAL_DIM // GLA_HEADS
GLA_LOW_RANK = 16
GLA_GATE_NORMALIZER = 16.0

SSM_D_INNER = 2 * D_MODEL
SSM_HEAD_DIM = 64
SSM_HEADS = SSM_D_INNER // SSM_HEAD_DIM
SSM_GROUPS = 4
SSM_HEADS_PER_GROUP = SSM_HEADS // SSM_GROUPS
SSM_STATE = 128
SSM_CONV_W = 4
SSM_BC_WIDTH = SSM_GROUPS * SSM_STATE
SSM_CONV_DIM = SSM_D_INNER + 2 * SSM_BC_WIDTH
SSM_GROUP_WIDTH = SSM_D_INNER // SSM_GROUPS

DIFF_HEADS = 8
DIFF_HEAD_DIM = 64
DIFF_V_DIM = 2 * DIFF_HEAD_DIM
DIFF_QK_WIDTH = DIFF_HEADS * 2 * DIFF_HEAD_DIM
DIFF_V_WIDTH = DIFF_HEADS * DIFF_V_DIM
ROT_DIM = DIFF_HEAD_DIM // 4
DIFF_SUBLN_EPS = 1e-5

N_BRANCHES = 3
D_FF = 4 * D_MODEL

LANES = 128
MASK_VALUE = -1e30
QK_SCALE_LOG2 = DIFF_HEAD_DIM ** -0.5 * math.log2(math.e)

COL_XBC = 0
COL_GATE = COL_XBC + SSM_CONV_DIM
COL_GLA_V = COL_GATE + N_BRANCHES * D_MODEL
COL_GLA_G = COL_GLA_V + GLA_VAL_DIM
COL_DIFF_Q = COL_GLA_G + GLA_VAL_DIM
COL_DIFF_K = COL_DIFF_Q + DIFF_QK_WIDTH
COL_DIFF_V = COL_DIFF_K + DIFF_QK_WIDTH
COL_GLA_Q = COL_DIFF_V + DIFF_V_WIDTH
COL_GLA_K = COL_GLA_Q + GLA_KEY_DIM
COL_SSM_Z = COL_GLA_K + GLA_KEY_DIM
SLAB_COLS = COL_SSM_Z + SSM_D_INNER
SMALL_COLS = 2 * LANES

VMEM_LIMIT = 56 * 1024 * 1024


def _cparams(sem):
    return pltpu.CompilerParams(dimension_semantics=sem, vmem_limit_bytes=VMEM_LIMIT)


def _dot(a, b):
    return jnp.dot(a, b, preferred_element_type=F32)


def _dot_nt(a, b):
    return lax.dot_general(a, b, (((1,), (1,)), ((), ())), preferred_element_type=F32)


def _split3(x):
    hi = x.astype(BF16)
    r1 = x - hi.astype(F32)
    mid = r1.astype(BF16)
    lo = (r1 - mid.astype(F32)).astype(BF16)
    return hi, mid, lo


def _dot_sel_lhs(sel, x):
    hi, mid, lo = _split3(x)
    return _dot(sel, hi) + _dot(sel, mid) + _dot(sel, lo)


def _dot_sel_rhs(x, sel):
    hi, mid, lo = _split3(x)
    return _dot(hi, sel) + _dot(mid, sel) + _dot(lo, sel)


def _dot_nt_sel_lhs(sel, x):
    hi, mid, lo = _split3(x)
    return _dot_nt(sel, hi) + _dot_nt(sel, mid) + _dot_nt(sel, lo)


def _dot_f32(a, b):
    ah, am, _ = _split3(a)
    bh, bm, _ = _split3(b)
    return _dot(ah, bh) + _dot(ah, bm) + _dot(am, bh)


def _silu(x):
    return x / (1.0 + jnp.exp(-x))


def _log1p_exp_neg_abs(x):
    return jnp.log(1.0 + jnp.exp(-jnp.abs(x)))


def _log_sigmoid(x):
    return jnp.minimum(x, 0.0) - _log1p_exp_neg_abs(x)


def _softplus(x):
    return jnp.maximum(x, 0.0) + _log1p_exp_neg_abs(x)


def _iota(shape, dim):
    return lax.broadcasted_iota(jnp.int32, shape, dim)


def _inproj_kernel(x_ref, g_ref, w_ref, ws_ref, o_ref, os_ref, h_ref):
    @pl.when(pl.program_id(1) == 0)
    def _():
        x = x_ref[...]
        ms = jnp.mean(x * x, axis=-1, keepdims=True)
        h = (x * lax.rsqrt(ms + EPS) * g_ref[...]).astype(BF16)
        h_ref[...] = h
        os_ref[...] = _dot(h, ws_ref[...])

    o_ref[...] = _dot(h_ref[...], w_ref[...]).astype(o_ref.dtype)


def _inproj(x2, g, w_slab, w_small, tm=1024, tn=1024):
    m = x2.shape[0]
    return pl.pallas_call(
        _inproj_kernel,
        grid=(m // tm, SLAB_COLS // tn),
        in_specs=[
            pl.BlockSpec((tm, D_MODEL), lambda i, j: (i, 0)),
            pl.BlockSpec((1, D_MODEL), lambda i, j: (0, 0)),
            pl.BlockSpec((D_MODEL, tn), lambda i, j: (0, j)),
            pl.BlockSpec((D_MODEL, SMALL_COLS), lambda i, j: (0, 0)),
        ],
        out_specs=[
            pl.BlockSpec((tm, tn), lambda i, j: (i, j)),
            pl.BlockSpec((tm, SMALL_COLS), lambda i, j: (i, 0)),
        ],
        out_shape=[
            jax.ShapeDtypeStruct((m, SLAB_COLS), BF16),
            jax.ShapeDtypeStruct((m, SMALL_COLS), F32),
        ],
        scratch_shapes=[pltpu.VMEM((tm, D_MODEL), BF16)],
        compiler_params=_cparams(("parallel", "arbitrary")),
        name="inproj",
    )(x2, g, w_slab, w_small)


def _level_anchor(b, h):
    c = CHUNK
    if 2 * h >= 8:
        parts = [jnp.broadcast_to(b[s + h - 1:s + h, :], (2 * h, b.shape[1]))
                 for s in range(0, c, 2 * h)]
        return parts[0] if len(parts) == 1 else jnp.concatenate(parts, axis=0)
    pos = _iota(b.shape, 0) % (2 * h)
    if h == 1:
        return jnp.where(pos == 1, pltpu.roll(b, 1, 0), b)
    return jnp.where(pos == 0, pltpu.roll(b, c - 1, 0),
                     jnp.where(pos == 1, b,
                               jnp.where(pos == 2, pltpu.roll(b, 1, 0), pltpu.roll(b, 2, 0))))


def _gla_kernel(q_ref, k_ref, v_ref, g_ref, sm_ref, wgk_ref, bgk_ref, ng_ref, o_ref, st_ref):
    c = CHUNK

    @pl.when(pl.program_id(1) == 0)
    def _():
        st_ref[...] = jnp.zeros_like(st_ref)

    gk = _log_sigmoid(_dot_f32(sm_ref[...], wgk_ref[...]) + bgk_ref[...]) / GLA_GATE_NORMALIZER
    row = _iota((c, c), 0)
    col = _iota((c, c), 1)
    tril = jnp.where(col <= row, 1.0, 0.0).astype(BF16)
    b_all = _dot_sel_lhs(tril, gk)
    eye = row == col

    heads = range(GLA_HEADS)
    ksl = [slice(hd * GLA_DK, (hd + 1) * GLA_DK) for hd in heads]
    vsl = [slice(hd * GLA_DV, (hd + 1) * GLA_DV) for hd in heads]
    q = q_ref[...].astype(F32) * (GLA_DK ** -0.5)
    k = k_ref[...].astype(F32)
    b = b_all
    b_last = b[c - 1:c, :]
    st = [st_ref[hd] for hd in heads]

    qg = (q * jnp.exp(b)).astype(BF16)
    kd = (k * jnp.exp(b_last - b)).astype(BF16)
    e_last = jnp.exp(b_last)
    o = [_dot_nt(qg[:, ksl[hd]], st[hd].astype(BF16)) for hd in heads]
    for hd in heads:
        vt = v_ref[:, vsl[hd]].astype(F32).T.astype(BF16)
        st_ref[hd] = st[hd] * e_last[:, ksl[hd]] + _dot(vt, kd[:, ksl[hd]])

    q16 = q.astype(BF16)
    k16 = k.astype(BF16)
    scores = [jnp.where(eye, _dot_nt(q16[:, ksl[hd]], k16[:, ksl[hd]]), 0.0) for hd in heads]
    h = c // 2
    while h >= 1:
        anchor = _level_anchor(b, h)
        second = (_iota(b.shape, 0) % (2 * h)) >= h
        decay = jnp.exp(jnp.where(second, b - anchor, anchor - b))
        qt = jnp.where(second, q * decay, 0.0).astype(BF16)
        kt = jnp.where(second, 0.0, k * decay).astype(BF16)
        same_block = (row // (2 * h)) == (col // (2 * h))
        for hd in heads:
            p = _dot_nt(qt[:, ksl[hd]], kt[:, ksl[hd]])
            if 2 * h < c:
                p = jnp.where(same_block, p, 0.0)
            scores[hd] = scores[hd] + p
        h //= 2

    for hd in heads:
        oh = o[hd] + _dot(scores[hd].astype(BF16), v_ref[:, vsl[hd]])
        g = g_ref[:, vsl[hd]].astype(F32)
        ms = jnp.mean(oh * oh, axis=-1, keepdims=True)
        o_ref[:, vsl[hd]] = (oh * lax.rsqrt(ms + EPS) * ng_ref[...] * _silu(g)).astype(o_ref.dtype)


def _gla(slab, small, wgk_pad, b_gk, norm_g, bsz, s_len):
    nc = s_len // CHUNK
    rb = lambda b, i: b * nc + i
    return pl.pallas_call(
        _gla_kernel,
        grid=(bsz, nc),
        in_specs=[
            pl.BlockSpec((CHUNK, GLA_KEY_DIM), lambda b, i: (rb(b, i), COL_GLA_Q // GLA_KEY_DIM)),
            pl.BlockSpec((CHUNK, GLA_KEY_DIM), lambda b, i: (rb(b, i), COL_GLA_K // GLA_KEY_DIM)),
            pl.BlockSpec((CHUNK, GLA_VAL_DIM), lambda b, i: (rb(b, i), COL_GLA_V // GLA_VAL_DIM)),
            pl.BlockSpec((CHUNK, GLA_VAL_DIM), lambda b, i: (rb(b, i), COL_GLA_G // GLA_VAL_DIM)),
            pl.BlockSpec((CHUNK, LANES), lambda b, i: (rb(b, i), 0)),
            pl.BlockSpec((LANES, GLA_KEY_DIM), lambda b, i: (0, 0)),
            pl.BlockSpec((1, GLA_KEY_DIM), lambda b, i: (0, 0)),
            pl.BlockSpec((1, GLA_DV), lambda b, i: (0, 0)),
        ],
        out_specs=pl.BlockSpec((CHUNK, GLA_VAL_DIM), lambda b, i: (rb(b, i), 0)),
        out_shape=jax.ShapeDtypeStruct((bsz * s_len, GLA_VAL_DIM), BF16),
        scratch_shapes=[pltpu.VMEM((GLA_HEADS, GLA_DV, GLA_DK), F32)],
        compiler_params=_cparams(("parallel", "arbitrary")),
        name="gla",
    )(slab, slab, slab, slab, small, wgk_pad, b_gk, norm_g)


def _ssd_kernel(xbc_ref, z_ref, sm_ref, cw_ref, cb_ref, dtb_ref, alog_ref, dexp_ref, ng_ref,
                expand_ref, pairsel_ref, o_ref, xpad_ref, st_ref, yd_ref):
    c = CHUNK
    n = SSM_STATE
    gw = SSM_GROUP_WIDTH
    pad = 8

    @pl.when(pl.program_id(1) == 0)
    def _():
        st_ref[...] = jnp.zeros_like(st_ref)
        xpad_ref[0:pad, :] = jnp.zeros((pad, SSM_CONV_DIM), F32)

    xpad_ref[pad:pad + c, :] = xbc_ref[...].astype(F32)
    conv = cb_ref[...]
    for i in range(SSM_CONV_W):
        off = pad - (SSM_CONV_W - 1) + i
        conv = conv + xpad_ref[off:off + c, :] * cw_ref[i:i + 1, :]
    xpad_ref[0:pad, :] = xpad_ref[c:c + pad, :]
    xc = _silu(conv)
    xs = xc[:, :SSM_D_INNER]

    dt = _softplus(sm_ref[...] + dtb_ref[...])
    da = dt * (-jnp.exp(alog_ref[...]))
    row = _iota((c, c), 0)
    col = _iota((c, c), 1)
    tril = jnp.where(col <= row, 1.0, 0.0).astype(BF16)
    a_cum = _dot_sel_lhs(tril, da)

    both = _dot_sel_rhs(jnp.concatenate([dt, a_cum], axis=0), expand_ref[...])
    dt_e = both[:c]
    a_e = both[c:]
    a_last_e = a_e[c - 1:c, :]
    xdt = xs * dt_e
    decay_out = jnp.exp(a_e)
    xdtd = (xdt * jnp.exp(a_last_e - a_e)).astype(BF16)
    chunk_decay = jnp.exp(a_last_e)

    lane = _iota((c, LANES), 1)
    z0 = jnp.concatenate([jnp.where(lane % 2 == 0, a_cum, 0.0),
                          jnp.where(lane % 2 == 1, a_cum, 0.0)], axis=0)
    pairs = _dot_nt_sel_lhs(pairsel_ref[...], z0)

    lane2 = _iota((c, 2 * c), 1)
    row2 = _iota((c, 2 * c), 0)
    causal2 = (lane2 % c) <= row2
    bd_row = _iota((2 * c, 2 * c), 0)
    bd_col = _iota((2 * c, 2 * c), 1)
    bd_mask = (bd_row // c) == (bd_col // c)

    for g in range(SSM_GROUPS):
        bg = xc[:, SSM_D_INNER + g * n:SSM_D_INNER + (g + 1) * n]
        cg = xc[:, SSM_D_INNER + SSM_BC_WIDTH + g * n:SSM_D_INNER + SSM_BC_WIDTH + (g + 1) * n]
        cg16 = cg.astype(BF16)
        bg16 = bg.astype(BF16)
        gsl = slice(g * gw, (g + 1) * gw)
        st = st_ref[g]

        y_off = _dot(cg16, st.astype(BF16)) * decay_out[:, gsl]
        st_ref[g] = st * chunk_decay[:, gsl] + _dot(bg.T.astype(BF16), xdtd[:, gsl])

        cb2 = _dot_nt(cg16, jnp.concatenate([bg16, bg16], axis=0))
        for pr in range(SSM_HEADS_PER_GROUP // 2):
            pidx = g * (SSM_HEADS_PER_GROUP // 2) + pr
            lsl = slice(pidx * 2 * c, (pidx + 1) * 2 * c)
            seg = a_e[:, lsl] - pairs[pidx:pidx + 1, :]
            lmat = jnp.exp(jnp.where(causal2, seg, MASK_VALUE))
            w = (cb2 * lmat).astype(BF16)
            xp = xdt[:, lsl]
            bd = jnp.where(bd_mask, jnp.concatenate([xp, xp], axis=0), 0.0).astype(BF16)
            yd_ref[:, lsl] = _dot(w, bd)

        y = yd_ref[:, gsl] + y_off + xs[:, gsl] * dexp_ref[:, gsl]
        y = y * _silu(z_ref[:, gsl].astype(F32))
        ms = jnp.mean(y * y, axis=-1, keepdims=True)
        o_ref[:, gsl] = (y * lax.rsqrt(ms + EPS) * ng_ref[:, gsl]).astype(o_ref.dtype)


def _ssd(slab, small, conv_w, conv_b, dtb_pad, alog_pad, d_exp, norm_g, expand, pairsel, bsz, s_len):
    nc = s_len // CHUNK
    rb = lambda b, i: b * nc + i
    const = lambda b, i: (0, 0)
    return pl.pallas_call(
        _ssd_kernel,
        grid=(bsz, nc),
        in_specs=[
            pl.BlockSpec((CHUNK, SSM_CONV_DIM), lambda b, i: (rb(b, i), COL_XBC // SSM_CONV_DIM)),
            pl.BlockSpec((CHUNK, SSM_D_INNER), lambda b, i: (rb(b, i), COL_SSM_Z // SSM_D_INNER)),
            pl.BlockSpec((CHUNK, LANES), lambda b, i: (rb(b, i), 1)),
            pl.BlockSpec((SSM_CONV_W, SSM_CONV_DIM), const),
            pl.BlockSpec((1, SSM_CONV_DIM), const),
            pl.BlockSpec((1, LANES), const),
            pl.BlockSpec((1, LANES), const),
            pl.BlockSpec((1, SSM_D_INNER), const),
            pl.BlockSpec((1, SSM_D_INNER), const),
            pl.BlockSpec((LANES, SSM_D_INNER), const),
            pl.BlockSpec((SSM_HEADS // 2, LANES), const),
        ],
        out_specs=pl.BlockSpec((CHUNK, SSM_D_INNER), lambda b, i: (rb(b, i), 0)),
        out_shape=jax.ShapeDtypeStruct((bsz * s_len, SSM_D_INNER), BF16),
        scratch_shapes=[
            pltpu.VMEM((CHUNK + 8, SSM_CONV_DIM), F32),
            pltpu.VMEM((SSM_GROUPS, SSM_STATE, SSM_GROUP_WIDTH), F32),
            pltpu.VMEM((CHUNK, SSM_D_INNER), F32),
        ],
        compiler_params=_cparams(("parallel", "arbitrary")),
        name="ssd",
    )(slab, slab, small, conv_w, conv_b, dtb_pad, alog_pad, d_exp, norm_g, expand, pairsel)


def _rope_table_kernel(pos_ref, freq_ref, c_ref, s1_ref, s2_ref):
    ang = pos_ref[...].astype(F32) * freq_ref[...]
    d = _iota(ang.shape, 1) % DIFF_HEAD_DIM
    half = ROT_DIM // 2
    cosv = jnp.cos(ang)
    sinv = jnp.sin(ang)
    c_ref[...] = jnp.where(d < ROT_DIM, cosv, 1.0)
    s1_ref[...] = jnp.where(d < half, -sinv, 0.0)
    s2_ref[...] = jnp.where((d >= half) & (d < ROT_DIM), sinv, 0.0)


def _rope_tables(pos_col, freq_row, tr=512):
    m = pos_col.shape[0]
    spec = pl.BlockSpec((tr, LANES), lambda i: (i, 0))
    shp = jax.ShapeDtypeStruct((m, LANES), F32)
    return pl.pallas_call(
        _rope_table_kernel,
        grid=(m // tr,),
        in_specs=[pl.BlockSpec((tr, 1), lambda i: (i, 0)), pl.BlockSpec((1, LANES), lambda i: (0, 0))],
        out_specs=[spec, spec, spec],
        out_shape=[shp, shp, shp],
        compiler_params=_cparams(("parallel",)),
        name="rope_tables",
    )(pos_col, freq_row)


def _attn_prep_kernel(q_ref, k_ref, v_ref, c_ref, s1_ref, s2_ref, qo_ref, ko_ref, vt_ref):
    half = ROT_DIM // 2
    cosf = c_ref[...]
    s1 = s1_ref[...]
    s2 = s2_ref[...]

    def rope(t):
        return t * cosf + pltpu.roll(t, LANES - half, 1) * s1 + pltpu.roll(t, half, 1) * s2

    for hd in range(DIFF_HEADS):
        sl = slice(hd * LANES, (hd + 1) * LANES)
        q = q_ref[:, sl].astype(F32)
        k = k_ref[:, sl].astype(F32)
        qo_ref[:, sl] = (rope(q) * QK_SCALE_LOG2).astype(BF16)
        ko_ref[:, sl] = rope(k).astype(BF16)
        vt_ref[0, hd, 0] = v_ref[:, sl].astype(F32).T.astype(BF16)


def _attn_prep(slab, cosf, s1, s2, bsz, s_len, tr):
    m = bsz * s_len
    nb = s_len // tr
    w = DIFF_QK_WIDTH
    row = pl.BlockSpec((tr, w), lambda i: (i, 0))
    tab = pl.BlockSpec((tr, LANES), lambda i: (i, 0))
    return pl.pallas_call(
        _attn_prep_kernel,
        grid=(m // tr,),
        in_specs=[
            pl.BlockSpec((tr, w), lambda i: (i, COL_DIFF_Q // w)),
            pl.BlockSpec((tr, w), lambda i: (i, COL_DIFF_K // w)),
            pl.BlockSpec((tr, w), lambda i: (i, COL_DIFF_V // w)),
            tab, tab, tab,
        ],
        out_specs=[
            row, row,
            pl.BlockSpec((1, DIFF_HEADS, 1, DIFF_V_DIM, tr), lambda i: (i // nb, 0, i % nb, 0, 0)),
        ],
        out_shape=[
            jax.ShapeDtypeStruct((m, w), BF16),
            jax.ShapeDtypeStruct((m, w), BF16),
            jax.ShapeDtypeStruct((bsz, DIFF_HEADS, nb, DIFF_V_DIM, tr), BF16),
        ],
        compiler_params=_cparams(("parallel",)),
        name="attn_prep",
    )(slab, slab, slab, cosf, s1, s2)


def _flash_kernel(q_ref, k_ref, vt_ref, lq1_ref, lk1_ref, lq2_ref, lk2_ref, ng_ref, o_ref,
                  sa_ref, sb_ref, *, tq, cw, lambda_init):
    qi = pl.program_id(2)
    nch = 2 * tq // cw
    lead = 1
    q = q_ref[...]
    lane = _iota(q.shape, 1)
    zero = jnp.zeros_like(q)
    qs = jnp.concatenate([jnp.where(lane < DIFF_HEAD_DIM, q, zero),
                          jnp.where(lane >= DIFF_HEAD_DIM, q, zero)], axis=0)
    q_chains = [qs[c * cw:(c + 1) * cw] for c in range(nch)]

    def keys(j):
        return k_ref[pl.ds(pl.multiple_of(j * tq, tq), tq), :]

    def scores(j, s_ref):
        kj = keys(j)
        for c in range(nch):
            s_ref[c] = _dot_nt(kj, q_chains[c])

    def absorb(j, s_ref, stats, masked, nxt_ref=None):
        vtj = vt_ref[0, 0, j]
        kn = None if nxt_ref is None else keys(j + 1)
        out = []
        if nxt_ref is not None:
            for c in range(min(lead, nch)):
                nxt_ref[c] = _dot_nt(kn, q_chains[c])
        for c in range(nch):
            if nxt_ref is not None and c + lead < nch:
                nxt_ref[c + lead] = _dot_nt(kn, q_chains[c + lead])
            m, l, acc = stats[c]
            s = s_ref[c]
            if masked:
                kc = _iota(s.shape, 0) // CHUNK
                qc = ((c * cw + _iota(s.shape, 1)) % tq) // CHUNK
                s = jnp.where(kc <= qc, s, MASK_VALUE)
            m_new = jnp.maximum(m, jnp.max(s, axis=0, keepdims=True))
            p = jnp.exp2(s - m_new)
            alpha = jnp.exp2(m - m_new)
            l = alpha * l + jnp.sum(p, axis=0, keepdims=True)
            acc = acc * alpha + _dot(vtj, p.astype(BF16))
            out.append((m_new, l, acc))
        return tuple(out)

    def pair(jj, stats):
        j = 2 * jj
        stats = absorb(j, sa_ref, stats, False, sb_ref)
        return absorb(j + 1, sb_ref, stats, False, sa_ref)

    def odd_tail(stats):
        stats = absorb(qi - 1, sa_ref, stats, False)
        scores(qi, sa_ref)
        return stats

    init = tuple((jnp.full((1, cw), MASK_VALUE, F32), jnp.zeros((1, cw), F32),
                  jnp.zeros((DIFF_V_DIM, cw), F32)) for _ in range(nch))
    scores(0, sa_ref)
    stats = lax.fori_loop(0, qi // 2, pair, init)
    stats = lax.cond(qi % 2 == 1, odd_tail, lambda st: st, stats)
    carry = absorb(qi, sa_ref, stats, True)

    lam = (jnp.exp(jnp.sum(lq1_ref[...] * lk1_ref[...], keepdims=True))
           - jnp.exp(jnp.sum(lq2_ref[...] * lk2_ref[...], keepdims=True)) + lambda_init)
    on = jnp.concatenate([acc / l for (_, l, acc) in carry], axis=1)
    o = on[:, :tq] - lam * on[:, tq:]
    ms = jnp.mean(o * o, axis=0, keepdims=True)
    o = o * lax.rsqrt(ms + DIFF_SUBLN_EPS)
    o_ref[...] = (o.T * ng_ref[...] * (1.0 - lambda_init)).astype(o_ref.dtype)


def _flash(q_r, k_r, v_t, lq1, lk1, lq2, lk2, norm_g, bsz, s_len, tq, lambda_init):
    nq = s_len // tq
    cw = min(256, tq)
    vec = pl.BlockSpec((1, DIFF_HEAD_DIM), lambda b, h, i: (0, 0))
    return pl.pallas_call(
        functools.partial(_flash_kernel, tq=tq, cw=cw, lambda_init=lambda_init),
        grid=(bsz, DIFF_HEADS, nq),
        in_specs=[
            pl.BlockSpec((tq, LANES), lambda b, h, i: (b * nq + i, h)),
            pl.BlockSpec((s_len, LANES), lambda b, h, i: (b, h)),
            pl.BlockSpec((1, 1, nq, DIFF_V_DIM, tq), lambda b, h, i: (b, h, 0, 0, 0)),
            vec, vec, vec, vec,
            pl.BlockSpec((1, DIFF_V_DIM), lambda b, h, i: (0, 0)),
        ],
        out_specs=pl.BlockSpec((tq, DIFF_V_DIM), lambda b, h, i: (b * nq + i, h)),
        out_shape=jax.ShapeDtypeStruct((bsz * s_len, DIFF_V_WIDTH), BF16),
        scratch_shapes=[pltpu.VMEM((2 * tq // cw, tq, cw), F32),
                        pltpu.VMEM((2 * tq // cw, tq, cw), F32)],
        compiler_params=_cparams(("parallel", "parallel", "arbitrary")),
        name="flash",
    )(q_r, k_r, v_t, lq1, lk1, lq2, lk2, norm_g)


def _merge_kernel(x_ref, yg_ref, ys_ref, yd_ref, g0_ref, g1_ref, g2_ref, bg_ref,
                  wg_ref, ws_ref, wd_ref, wo_ref, o_ref):
    def gate(g_ref, i):
        return 1.0 / (1.0 + jnp.exp(-(g_ref[...].astype(F32) + bg_ref[i:i + 1, :])))

    mixed = gate(g0_ref, 0) * _dot(yg_ref[...], wg_ref[...])
    mixed = mixed + gate(g1_ref, 1) * _dot(ys_ref[...], ws_ref[...])
    mixed = mixed + gate(g2_ref, 2) * _dot(yd_ref[...], wd_ref[...])
    o_ref[...] = x_ref[...] + _dot(mixed.astype(BF16), wo_ref[...])


def _merge(x2, y_gla, y_ssm, y_diff, slab, b_gate, w_g, w_s, w_d, w_o, tm=512):
    m = x2.shape[0]
    gcol = COL_GATE // D_MODEL
    row = lambda w: pl.BlockSpec((tm, w), lambda i: (i, 0))
    const = lambda r, c: pl.BlockSpec((r, c), lambda i: (0, 0))
    return pl.pallas_call(
        _merge_kernel,
        grid=(m // tm,),
        in_specs=[
            row(D_MODEL), row(GLA_VAL_DIM), row(SSM_D_INNER), row(DIFF_V_WIDTH),
            pl.BlockSpec((tm, D_MODEL), lambda i: (i, gcol)),
            pl.BlockSpec((tm, D_MODEL), lambda i: (i, gcol + 1)),
            pl.BlockSpec((tm, D_MODEL), lambda i: (i, gcol + 2)),
            const(N_BRANCHES, D_MODEL),
            const(GLA_VAL_DIM, D_MODEL), const(SSM_D_INNER, D_MODEL), const(DIFF_V_WIDTH, D_MODEL),
            const(D_MODEL, D_MODEL),
        ],
        out_specs=row(D_MODEL),
        out_shape=jax.ShapeDtypeStruct((m, D_MODEL), F32),
        compiler_params=_cparams(("parallel",)),
        name="merge",
    )(x2, y_gla, y_ssm, y_diff, slab, slab, slab, b_gate, w_g, w_s, w_d, w_o)


def _mlp_kernel(x_ref, g_ref, wu_ref, wd_ref, gf_ref, o_ref, *, final_norm, fc):
    x = x_ref[...]
    ms = jnp.mean(x * x, axis=-1, keepdims=True)
    h = (x * lax.rsqrt(ms + EPS) * g_ref[...]).astype(BF16)
    acc = x
    for c0 in range(0, D_FF, fc):
        u = jnp.maximum(_dot(h, wu_ref[:, c0:c0 + fc]), 0.0)
        acc = acc + _dot((u * u).astype(BF16), wd_ref[c0:c0 + fc, :])
    if final_norm:
        ms = jnp.mean(acc * acc, axis=-1, keepdims=True)
        acc = acc * lax.rsqrt(ms + EPS) * gf_ref[...]
    o_ref[...] = acc


def _mlp(x2, g, w_up, w_down, g_final, final_norm, tm=512, fc=1024):
    m = x2.shape[0]
    row = pl.BlockSpec((tm, D_MODEL), lambda i: (i, 0))
    const = lambda r, c: pl.BlockSpec((r, c), lambda i: (0, 0))
    return pl.pallas_call(
        functools.partial(_mlp_kernel, final_norm=final_norm, fc=fc),
        grid=(m // tm,),
        in_specs=[row, const(1, D_MODEL), const(D_MODEL, D_FF), const(D_FF, D_MODEL), const(1, D_MODEL)],
        out_specs=row,
        out_shape=jax.ShapeDtypeStruct((m, D_MODEL), F32),
        compiler_params=_cparams(("parallel",)),
        name="mlp",
    )(x2, g, w_up, w_down, g_final)


def _slab_weights(w_in_l):
    sizes = (GLA_KEY_DIM, GLA_KEY_DIM, GLA_VAL_DIM, GLA_LOW_RANK, GLA_VAL_DIM,
             SSM_D_INNER, SSM_CONV_DIM, SSM_HEADS,
             DIFF_QK_WIDTH, DIFF_QK_WIDTH, DIFF_V_WIDTH, N_BRANCHES * D_MODEL)
    offs = np.concatenate([[0], np.cumsum(sizes)])
    (a_q, a_k, a_v, a_gk, a_g, b_z, b_xbc, b_dt, c_q, c_k, c_v, gate) = [
        w_in_l[:, int(offs[i]):int(offs[i + 1])] for i in range(len(sizes))]
    slab = jnp.concatenate([b_xbc, gate, a_v, a_g, c_q, c_k, c_v, a_q, a_k, b_z], axis=1).astype(BF16)
    zeros = lambda n: jnp.zeros((D_MODEL, n), w_in_l.dtype)
    small = jnp.concatenate([a_gk, zeros(LANES - GLA_LOW_RANK), b_dt, zeros(LANES - SSM_HEADS)],
                            axis=1).astype(BF16)
    return slab, small


def _pad_lanes(v, fill=0.0):
    v = v.reshape(1, -1).astype(F32)
    return jnp.pad(v, ((0, 0), (0, LANES - v.shape[1])), constant_values=fill)


def _forward(x, positions, norm_mix_g, w_in, b_gate, gla_w_gk2, gla_b_gk, gla_norm_g,
             ssm_conv_w, ssm_conv_b, ssm_dt_bias, ssm_a_log, ssm_d, ssm_norm_g,
             diff_lq1, diff_lk1, diff_lq2, diff_lk2, diff_norm_g,
             w_br_gla, w_br_ssm, w_br_diff, w_out, norm_mlp_g, w_mlp_up, w_mlp_down,
             norm_final_g, tq):
    bsz, s_len, _ = x.shape
    m = bsz * s_len
    depth = w_in.shape[0]
    x2 = x.reshape(m, D_MODEL).astype(F32)

    head_of_lane = np.arange(SSM_D_INNER) // SSM_HEAD_DIM
    expand = jnp.asarray((np.arange(LANES)[:, None] == head_of_lane[None, :]), BF16)
    pairsel = jnp.asarray((np.arange(LANES)[None, :] // 2) == np.arange(SSM_HEADS // 2)[:, None], BF16)
    pairsel = pairsel * jnp.asarray(np.arange(LANES)[None, :] < SSM_HEADS, BF16)
    d = np.arange(LANES) % DIFF_HEAD_DIM
    inv_freq = ROPE_THETA ** (-(2.0 * (d % (ROT_DIM // 2))) / ROT_DIM)
    freq_row = jnp.asarray(np.where(d < ROT_DIM, inv_freq, 0.0).reshape(1, LANES), F32)

    cosf, s1, s2 = _rope_tables(positions.reshape(m, 1), freq_row, tr=min(512, s_len))

    for l in range(depth):
        w_slab, w_small = _slab_weights(w_in[l])
        slab, small = _inproj(x2, norm_mix_g[l].reshape(1, -1), w_slab, w_small,
                              tm=min(1024, m))

        wgk_pad = jnp.pad(gla_w_gk2[l].astype(F32), ((0, LANES - GLA_LOW_RANK), (0, 0)))
        y_gla = _gla(slab, small, wgk_pad, gla_b_gk[l].reshape(1, -1), gla_norm_g[l].reshape(1, -1),
                     bsz, s_len)

        d_exp = jnp.repeat(ssm_d[l].astype(F32), SSM_HEAD_DIM).reshape(1, -1)
        y_ssm = _ssd(slab, small, ssm_conv_w[l], ssm_conv_b[l].reshape(1, -1),
                     _pad_lanes(ssm_dt_bias[l]), _pad_lanes(ssm_a_log[l]), d_exp,
                     ssm_norm_g[l].reshape(1, -1), expand, pairsel, bsz, s_len)

        lambda_init = 0.8 - 0.6 * math.exp(-0.3 * l)
        q_r, k_r, v_t = _attn_prep(slab, cosf, s1, s2, bsz, s_len, tq)
        y_diff = _flash(q_r, k_r, v_t, diff_lq1[l].reshape(1, -1), diff_lk1[l].reshape(1, -1),
                        diff_lq2[l].reshape(1, -1), diff_lk2[l].reshape(1, -1),
                        diff_norm_g[l].reshape(1, -1), bsz, s_len, tq, lambda_init)

        x2 = _merge(x2, y_gla, y_ssm, y_diff, slab, b_gate[l].reshape(N_BRANCHES, D_MODEL),
                    w_br_gla[l].astype(BF16), w_br_ssm[l].astype(BF16), w_br_diff[l].astype(BF16),
                    w_out[l].astype(BF16), tm=min(512, m))
        x2 = _mlp(x2, norm_mlp_g[l].reshape(1, -1), w_mlp_up[l].astype(BF16),
                  w_mlp_down[l].astype(BF16), norm_final_g.reshape(1, -1),
                  final_norm=(l == depth - 1), tm=min(512, m))
    return x2.reshape(bsz, s_len, D_MODEL)


def kernel(x, positions, norm_mix_g, w_in, b_gate, gla_w_gk2, gla_b_gk, gla_norm_g, ssm_conv_w,
           ssm_conv_b, ssm_dt_bias, ssm_a_log, ssm_d, ssm_norm_g, diff_lq1, diff_lk1, diff_lq2,
           diff_lk2, diff_norm_g, w_br_gla, w_br_ssm, w_br_diff, w_out, norm_mlp_g, w_mlp_up,
           w_mlp_down, norm_final_g):
    return _forward(x, positions, norm_mix_g, w_in, b_gate, gla_w_gk2, gla_b_gk, gla_norm_g,
                    ssm_conv_w, ssm_conv_b, ssm_dt_bias, ssm_a_log, ssm_d, ssm_norm_g,
                    diff_lq1, diff_lk1, diff_lq2, diff_lk2, diff_norm_g,
                    w_br_gla, w_br_ssm, w_br_diff, w_out, norm_mlp_g, w_mlp_up, w_mlp_down,
                    norm_final_g, tq=min(512, x.shape[1]))
```

```python
import functools
import math

import jax
import jax.numpy as jnp
import numpy as np
from jax import lax
from jax.experimental import pallas as pl
from jax.experimental.pallas import tpu as pltpu

F32 = jnp.float32
BF16 = jnp.bfloat16

D_MODEL = 1024
DEPTH = 2
CHUNK = 64
ROPE_THETA = 500000.0
EPS = 1e-6

GLA_HEADS = 4
GLA_KEY_DIM = D_MODEL // 2
GLA_VAL_DIM = D_MODEL
GLA_DK = GLA_KEY_DIM // GLA_HEADS
GLA_DV = GLA_VAL_DIM // GLA_HEADS
GLA_LOW_RANK = 16
GLA_GATE_NORMALIZER = 16.0

SSM_D_INNER = 2 * D_MODEL
SSM_HEAD_DIM = 64
SSM_HEADS = SSM_D_INNER // SSM_HEAD_DIM
SSM_GROUPS = 4
SSM_HEADS_PER_GROUP = SSM_HEADS // SSM_GROUPS
SSM_STATE = 128
SSM_CONV_W = 4
SSM_BC_WIDTH = SSM_GROUPS * SSM_STATE
SSM_CONV_DIM = SSM_D_INNER + 2 * SSM_BC_WIDTH
SSM_GROUP_WIDTH = SSM_D_INNER // SSM_GROUPS
SSM_CONV_PAD = 16
RECUR_CHUNKS_PER_STEP = 4

DIFF_HEADS = 8
DIFF_HEAD_DIM = 64
DIFF_V_DIM = 2 * DIFF_HEAD_DIM
DIFF_QK_WIDTH = DIFF_HEADS * 2 * DIFF_HEAD_DIM
DIFF_V_WIDTH = DIFF_HEADS * DIFF_V_DIM
ROT_DIM = DIFF_HEAD_DIM // 4
DIFF_SUBLN_EPS = 1e-5

N_BRANCHES = 3
D_FF = 4 * D_MODEL

LANES = 128
MASK_VALUE = -1e30
QK_SCALE_LOG2 = DIFF_HEAD_DIM ** -0.5 * math.log2(math.e)

COL_XBC = 0
COL_GATE = COL_XBC + SSM_CONV_DIM
COL_GLA_V = COL_GATE + N_BRANCHES * D_MODEL
COL_GLA_G = COL_GLA_V + GLA_VAL_DIM
COL_DIFF_Q = COL_GLA_G + GLA_VAL_DIM
COL_DIFF_K = COL_DIFF_Q + DIFF_QK_WIDTH
COL_DIFF_V = COL_DIFF_K + DIFF_QK_WIDTH
COL_GLA_Q = COL_DIFF_V + DIFF_V_WIDTH
COL_GLA_K = COL_GLA_Q + GLA_KEY_DIM
COL_SSM_Z = COL_GLA_K + GLA_KEY_DIM
SLAB_COLS = COL_SSM_Z + SSM_D_INNER
SMALL_COLS = 2 * LANES

VMEM_LIMIT = 56 * 1024 * 1024


def _cparams(sem):
    return pltpu.CompilerParams(dimension_semantics=sem, vmem_limit_bytes=VMEM_LIMIT)


def _dot(a, b):
    return jnp.dot(a, b, preferred_element_type=F32)


def _dot_nt(a, b):
    return lax.dot_general(a, b, (((1,), (1,)), ((), ())), preferred_element_type=F32)


def _split3(x):
    hi = x.astype(BF16)
    r1 = x - hi.astype(F32)
    mid = r1.astype(BF16)
    lo = (r1 - mid.astype(F32)).astype(BF16)
    return hi, mid, lo


def _dot_sel_lhs(sel, x):
    hi, mid, lo = _split3(x)
    return _dot(sel, hi) + _dot(sel, mid) + _dot(sel, lo)


def _dot_sel_rhs(x, sel):
    hi, mid, lo = _split3(x)
    return _dot(hi, sel) + _dot(mid, sel) + _dot(lo, sel)


def _dot_sel_rhs2(x, sel):
    hi = x.astype(BF16)
    mid = (x - hi.astype(F32)).astype(BF16)
    return _dot(hi, sel) + _dot(mid, sel)


def _dot_nt_sel_lhs(sel, x):
    hi, mid, lo = _split3(x)
    return _dot_nt(sel, hi) + _dot_nt(sel, mid) + _dot_nt(sel, lo)


def _dot_f32(a, b):
    ah, am, _ = _split3(a)
    bh, bm, _ = _split3(b)
    return _dot(ah, bh) + _dot(ah, bm) + _dot(am, bh)


def _silu(x):
    return x / (1.0 + jnp.exp(-x))


def _log1p_exp_neg_abs(x):
    return jnp.log(1.0 + jnp.exp(-jnp.abs(x)))


def _log_sigmoid(x):
    return jnp.minimum(x, 0.0) - _log1p_exp_neg_abs(x)


def _softplus(x):
    return jnp.maximum(x, 0.0) + _log1p_exp_neg_abs(x)


def _iota(shape, dim):
    return lax.broadcasted_iota(jnp.int32, shape, dim)


def _inproj_kernel(x_ref, g_ref, w_ref, ws_ref, o_ref, os_ref, h_ref):
    @pl.when(pl.program_id(1) == 0)
    def _():
        x = x_ref[...]
        ms = jnp.mean(x * x, axis=-1, keepdims=True)
        h = (x * lax.rsqrt(ms + EPS) * g_ref[...]).astype(BF16)
        h_ref[...] = h
        os_ref[...] = _dot(h, ws_ref[...])

    o_ref[...] = _dot(h_ref[...], w_ref[...]).astype(o_ref.dtype)


def _inproj(x2, g, w_slab, w_small, layer, tm=1024, tn=2048):
    m = x2.shape[0]
    return pl.pallas_call(
        _inproj_kernel,
        grid=(m // tm, SLAB_COLS // tn),
        in_specs=[
            pl.BlockSpec((tm, D_MODEL), lambda i, j: (i, 0)),
            pl.BlockSpec((1, D_MODEL), lambda i, j: (0, 0)),
            pl.BlockSpec((None, D_MODEL, tn), lambda i, j: (layer, 0, j)),
            pl.BlockSpec((None, D_MODEL, SMALL_COLS), lambda i, j: (layer, 0, 0)),
        ],
        out_specs=[
            pl.BlockSpec((tm, tn), lambda i, j: (i, j)),
            pl.BlockSpec((tm, SMALL_COLS), lambda i, j: (i, 0)),
        ],
        out_shape=[
            jax.ShapeDtypeStruct((m, SLAB_COLS), BF16),
            jax.ShapeDtypeStruct((m, SMALL_COLS), F32),
        ],
        scratch_shapes=[pltpu.VMEM((tm, D_MODEL), BF16)],
        compiler_params=_cparams(("parallel", "arbitrary")),
        name="inproj",
    )(x2, g, w_slab, w_small)


def _gates_kernel(sm_ref, wgk_ref, bgk_ref, dtb_ref, alog_ref, b_ref, dt_ref, acum_ref):
    rows = sm_ref.shape[0]
    r = _iota((rows, rows), 0)
    c = _iota((rows, rows), 1)
    tril = jnp.where((c <= r) & ((r // CHUNK) == (c // CHUNK)), 1.0, 0.0).astype(BF16)

    gk = _log_sigmoid(_dot_f32(sm_ref[:, :LANES], wgk_ref[...]) + bgk_ref[...]) / GLA_GATE_NORMALIZER
    b_ref[...] = _dot_sel_lhs(tril, gk)

    dt = _softplus(sm_ref[:, LANES:] + dtb_ref[...])
    dt_ref[...] = dt
    acum_ref[...] = _dot_sel_lhs(tril, dt * (-jnp.exp(alog_ref[...])))


def _gates(small, wgk_pad, b_gk, dtb_pad, alog_pad, rows=256):
    m = small.shape[0]
    rows = min(rows, m)
    const = lambda r, c: pl.BlockSpec((r, c), lambda i: (0, 0))
    out = lambda w: pl.BlockSpec((rows, w), lambda i: (i, 0))
    return pl.pallas_call(
        _gates_kernel,
        grid=(m // rows,),
        in_specs=[pl.BlockSpec((rows, SMALL_COLS), lambda i: (i, 0)),
                  const(LANES, GLA_KEY_DIM), const(1, GLA_KEY_DIM), const(1, LANES), const(1, LANES)],
        out_specs=[out(GLA_KEY_DIM), out(LANES), out(LANES)],
        out_shape=[jax.ShapeDtypeStruct((m, GLA_KEY_DIM), F32),
                   jax.ShapeDtypeStruct((m, LANES), F32),
                   jax.ShapeDtypeStruct((m, LANES), F32)],
        compiler_params=_cparams(("parallel",)),
        name="gates",
    )(small, wgk_pad, b_gk, dtb_pad, alog_pad)


def _level_anchor(b, h):
    c = CHUNK
    if 2 * h >= 8:
        parts = [jnp.broadcast_to(b[s + h - 1:s + h, :], (2 * h, b.shape[1]))
                 for s in range(0, c, 2 * h)]
        return parts[0] if len(parts) == 1 else jnp.concatenate(parts, axis=0)
    pos = _iota(b.shape, 0) % (2 * h)
    if h == 1:
        return jnp.where(pos == 1, pltpu.roll(b, 1, 0), b)
    return jnp.where(pos == 0, pltpu.roll(b, c - 1, 0),
                     jnp.where(pos == 1, b,
                               jnp.where(pos == 2, pltpu.roll(b, 1, 0), pltpu.roll(b, 2, 0))))


def _gla_kernel(q_ref, k_ref, v_ref, g_ref, b_ref, ng_ref, o_ref, st_ref):
    @pl.when(pl.program_id(1) == 0)
    def _():
        st_ref[...] = jnp.zeros_like(st_ref)

    for ci in range(q_ref.shape[0] // CHUNK):
        rows = slice(ci * CHUNK, (ci + 1) * CHUNK)
        _gla_chunk(q_ref.at[rows], k_ref.at[rows], v_ref.at[rows], g_ref.at[rows], b_ref.at[rows],
                   ng_ref, o_ref.at[rows], st_ref)


def _gla_chunk(q_ref, k_ref, v_ref, g_ref, b_ref, ng_ref, o_ref, st_ref):
    c = CHUNK
    b_all = b_ref[...]
    row = _iota((c, c), 0)
    col = _iota((c, c), 1)
    eye = row == col

    heads = range(GLA_HEADS)
    ksl = [slice(hd * GLA_DK, (hd + 1) * GLA_DK) for hd in heads]
    vsl = [slice(hd * GLA_DV, (hd + 1) * GLA_DV) for hd in heads]
    q = q_ref[...].astype(F32) * (GLA_DK ** -0.5)
    k = k_ref[...].astype(F32)
    b = b_all
    b_last = b[c - 1:c, :]
    st = [st_ref[hd] for hd in heads]

    qg = (q * jnp.exp(b)).astype(BF16)
    kd = (k * jnp.exp(b_last - b)).astype(BF16)
    e_last = jnp.exp(b_last)
    o = [_dot_nt(qg[:, ksl[hd]], st[hd].astype(BF16)) for hd in heads]
    for hd in heads:
        vt = v_ref[:, vsl[hd]].astype(F32).T.astype(BF16)
        st_ref[hd] = st[hd] * e_last[:, ksl[hd]] + _dot(vt, kd[:, ksl[hd]])

    q16 = q.astype(BF16)
    k16 = k.astype(BF16)
    scores = [jnp.where(eye, _dot_nt(q16[:, ksl[hd]], k16[:, ksl[hd]]), 0.0) for hd in heads]
    h = c // 2
    while h >= 1:
        anchor = _level_anchor(b, h)
        second = (_iota(b.shape, 0) % (2 * h)) >= h
        decay = jnp.exp(jnp.where(second, b - anchor, anchor - b))
        qt = jnp.where(second, q * decay, 0.0).astype(BF16)
        kt = jnp.where(second, 0.0, k * decay).astype(BF16)
        same_block = (row // (2 * h)) == (col // (2 * h))
        for hd in heads:
            p = _dot_nt(qt[:, ksl[hd]], kt[:, ksl[hd]])
            if 2 * h < c:
                p = jnp.where(same_block, p, 0.0)
            scores[hd] = scores[hd] + p
        h //= 2

    for hd in heads:
        oh = o[hd] + _dot(scores[hd].astype(BF16), v_ref[:, vsl[hd]])
        g = g_ref[:, vsl[hd]].astype(F32)
        ms = jnp.mean(oh * oh, axis=-1, keepdims=True)
        o_ref[:, vsl[hd]] = (oh * lax.rsqrt(ms + EPS) * ng_ref[...] * _silu(g)).astype(o_ref.dtype)


def _gla(slab, b_all, norm_g, bsz, s_len):
    tr = RECUR_CHUNKS_PER_STEP * CHUNK
    nc = s_len // tr
    rb = lambda b, i: b * nc + i
    return pl.pallas_call(
        _gla_kernel,
        grid=(bsz, nc),
        in_specs=[
            pl.BlockSpec((tr, GLA_KEY_DIM), lambda b, i: (rb(b, i), COL_GLA_Q // GLA_KEY_DIM)),
            pl.BlockSpec((tr, GLA_KEY_DIM), lambda b, i: (rb(b, i), COL_GLA_K // GLA_KEY_DIM)),
            pl.BlockSpec((tr, GLA_VAL_DIM), lambda b, i: (rb(b, i), COL_GLA_V // GLA_VAL_DIM)),
            pl.BlockSpec((tr, GLA_VAL_DIM), lambda b, i: (rb(b, i), COL_GLA_G // GLA_VAL_DIM)),
            pl.BlockSpec((tr, GLA_KEY_DIM), lambda b, i: (rb(b, i), 0)),
            pl.BlockSpec((1, GLA_DV), lambda b, i: (0, 0)),
        ],
        out_specs=pl.BlockSpec((tr, GLA_VAL_DIM), lambda b, i: (rb(b, i), 0)),
        out_shape=jax.ShapeDtypeStruct((bsz * s_len, GLA_VAL_DIM), BF16),
        scratch_shapes=[pltpu.VMEM((GLA_HEADS, GLA_DV, GLA_DK), F32)],
        compiler_params=_cparams(("parallel", "arbitrary")),
        name="gla",
    )(slab, slab, slab, slab, b_all, norm_g)


def _ssd_kernel(xbc_ref, z_ref, dt_ref, acum_ref, cw_ref, cb_ref, dexp_ref, ng_ref,
                expand_ref, pairsel_ref, shift_ref, o_ref, xpad_ref, st_ref, yd_ref):
    pad = SSM_CONV_PAD
    tr = xbc_ref.shape[0]

    @pl.when(pl.program_id(1) == 0)
    def _():
        st_ref[...] = jnp.zeros_like(st_ref)
        xpad_ref[0:pad, :] = jnp.zeros((pad, SSM_CONV_DIM), BF16)

    xpad_ref[pad:pad + tr, :] = xbc_ref[...]
    for ci in range(tr // CHUNK):
        rows = slice(ci * CHUNK, (ci + 1) * CHUNK)
        _ssd_chunk(xpad_ref.at[ci * CHUNK:(ci + 1) * CHUNK + pad], xbc_ref.at[rows], z_ref.at[rows],
                   dt_ref.at[rows], acum_ref.at[rows], cw_ref, cb_ref, dexp_ref, ng_ref,
                   expand_ref, pairsel_ref, shift_ref, o_ref.at[rows], st_ref, yd_ref.at[rows])
    xpad_ref[0:pad, :] = xpad_ref[tr:tr + pad, :]


def _ssd_chunk(xwin_ref, xbc_ref, z_ref, dt_ref, acum_ref, cw_ref, cb_ref, dexp_ref, ng_ref,
               expand_ref, pairsel_ref, shift_ref, o_ref, st_ref, yd_ref):
    c = CHUNK
    n = SSM_STATE
    gw = SSM_GROUP_WIDTH

    delayed = _dot(shift_ref[...], xwin_ref[...])
    conv = cb_ref[...] + xbc_ref[...].astype(F32) * cw_ref[SSM_CONV_W - 1:SSM_CONV_W, :]
    for i in range(SSM_CONV_W - 1):
        conv = conv + delayed[i * c:(i + 1) * c] * cw_ref[i:i + 1, :]
    xc = _silu(conv)
    xs = xc[:, :SSM_D_INNER]

    a_cum = acum_ref[...]
    both = _dot_sel_rhs2(jnp.concatenate([dt_ref[...], a_cum], axis=0), expand_ref[...])
    dt_e = both[:c]
    a_e = both[c:]
    a_last_e = a_e[c - 1:c, :]
    xdt = xs * dt_e
    decay_out = jnp.exp(a_e)
    xdtd = (xdt * jnp.exp(a_last_e - a_e)).astype(BF16)
    chunk_decay = jnp.exp(a_last_e)

    lane = _iota((c, LANES), 1)
    z0 = jnp.concatenate([jnp.where(lane % 2 == 0, a_cum, 0.0),
                          jnp.where(lane % 2 == 1, a_cum, 0.0)], axis=0)
    pairs = _dot_nt_sel_lhs(pairsel_ref[...], z0)

    lane2 = _iota((c, 2 * c), 1)
    row2 = _iota((c, 2 * c), 0)
    causal2 = (lane2 % c) <= row2
    bd_row = _iota((2 * c, 2 * c), 0)
    bd_col = _iota((2 * c, 2 * c), 1)
    bd_mask = (bd_row // c) == (bd_col // c)

    for g in range(SSM_GROUPS):
        bg = xc[:, SSM_D_INNER + g * n:SSM_D_INNER + (g + 1) * n]
        cg = xc[:, SSM_D_INNER + SSM_BC_WIDTH + g * n:SSM_D_INNER + SSM_BC_WIDTH + (g + 1) * n]
        cg16 = cg.astype(BF16)
        bg16 = bg.astype(BF16)
        gsl = slice(g * gw, (g + 1) * gw)
        st = st_ref[g]

        y_off = _dot(cg16, st.astype(BF16)) * decay_out[:, gsl]
        st_ref[g] = st * chunk_decay[:, gsl] + _dot(bg.T.astype(BF16), xdtd[:, gsl])

        cb2 = _dot_nt(cg16, jnp.concatenate([bg16, bg16], axis=0))
        for pr in range(SSM_HEADS_PER_GROUP // 2):
            pidx = g * (SSM_HEADS_PER_GROUP // 2) + pr
            lsl = slice(pidx * 2 * c, (pidx + 1) * 2 * c)
            seg = a_e[:, lsl] - pairs[pidx:pidx + 1, :]
            lmat = jnp.exp(jnp.where(causal2, seg, MASK_VALUE))
            w = (cb2 * lmat).astype(BF16)
            xp = xdt[:, lsl]
            bd = jnp.where(bd_mask, jnp.concatenate([xp, xp], axis=0), 0.0).astype(BF16)
            yd_ref[:, lsl] = _dot(w, bd)

        y = yd_ref[:, gsl] + y_off + xs[:, gsl] * dexp_ref[:, gsl]
        y = y * _silu(z_ref[:, gsl].astype(F32))
        ms = jnp.mean(y * y, axis=-1, keepdims=True)
        o_ref[:, gsl] = (y * lax.rsqrt(ms + EPS) * ng_ref[:, gsl]).astype(o_ref.dtype)


def _ssd(slab, dt, a_cum, conv_w, conv_b, d_exp, norm_g, expand, pairsel, shift, bsz, s_len):
    tr = RECUR_CHUNKS_PER_STEP * CHUNK
    nc = s_len // tr
    rb = lambda b, i: b * nc + i
    const = lambda b, i: (0, 0)
    return pl.pallas_call(
        _ssd_kernel,
        grid=(bsz, nc),
        in_specs=[
            pl.BlockSpec((tr, SSM_CONV_DIM), lambda b, i: (rb(b, i), COL_XBC // SSM_CONV_DIM)),
            pl.BlockSpec((tr, SSM_D_INNER), lambda b, i: (rb(b, i), COL_SSM_Z // SSM_D_INNER)),
            pl.BlockSpec((tr, LANES), lambda b, i: (rb(b, i), 0)),
            pl.BlockSpec((tr, LANES), lambda b, i: (rb(b, i), 0)),
            pl.BlockSpec((SSM_CONV_W, SSM_CONV_DIM), const),
            pl.BlockSpec((1, SSM_CONV_DIM), const),
            pl.BlockSpec((1, SSM_D_INNER), const),
            pl.BlockSpec((1, SSM_D_INNER), const),
            pl.BlockSpec((LANES, SSM_D_INNER), const),
            pl.BlockSpec((SSM_HEADS // 2, LANES), const),
            pl.BlockSpec(((SSM_CONV_W - 1) * CHUNK, CHUNK + SSM_CONV_PAD), const),
        ],
        out_specs=pl.BlockSpec((tr, SSM_D_INNER), lambda b, i: (rb(b, i), 0)),
        out_shape=jax.ShapeDtypeStruct((bsz * s_len, SSM_D_INNER), BF16),
        scratch_shapes=[
            pltpu.VMEM((tr + SSM_CONV_PAD, SSM_CONV_DIM), BF16),
            pltpu.VMEM((SSM_GROUPS, SSM_STATE, SSM_GROUP_WIDTH), F32),
            pltpu.VMEM((tr, SSM_D_INNER), F32),
        ],
        compiler_params=_cparams(("parallel", "arbitrary")),
        name="ssd",
    )(slab, slab, dt, a_cum, conv_w, conv_b, d_exp, norm_g, expand, pairsel, shift)


def _rope_table_kernel(pos_ref, freq_ref, c_ref, s1_ref, s2_ref):
    ang = pos_ref[...].astype(F32) * freq_ref[...]
    d = _iota(ang.shape, 1) % DIFF_HEAD_DIM
    half = ROT_DIM // 2
    cosv = jnp.cos(ang)
    sinv = jnp.sin(ang)
    c_ref[...] = jnp.where(d < ROT_DIM, cosv, 1.0)
    s1_ref[...] = jnp.where(d < half, -sinv, 0.0)
    s2_ref[...] = jnp.where((d >= half) & (d < ROT_DIM), sinv, 0.0)


def _rope_tables(pos_col, freq_row, tr=512):
    m = pos_col.shape[0]
    spec = pl.BlockSpec((tr, LANES), lambda i: (i, 0))
    shp = jax.ShapeDtypeStruct((m, LANES), F32)
    return pl.pallas_call(
        _rope_table_kernel,
        grid=(m // tr,),
        in_specs=[pl.BlockSpec((tr, 1), lambda i: (i, 0)), pl.BlockSpec((1, LANES), lambda i: (0, 0))],
        out_specs=[spec, spec, spec],
        out_shape=[shp, shp, shp],
        compiler_params=_cparams(("parallel",)),
        name="rope_tables",
    )(pos_col, freq_row)


def _attn_prep_kernel(q_ref, k_ref, v_ref, c_ref, s1_ref, s2_ref, qo_ref, ko_ref, vt_ref):
    half = ROT_DIM // 2
    cosf = c_ref[...]
    s1 = s1_ref[...]
    s2 = s2_ref[...]

    def rope(t):
        return t * cosf + pltpu.roll(t, LANES - half, 1) * s1 + pltpu.roll(t, half, 1) * s2

    for hd in range(DIFF_HEADS):
        sl = slice(hd * LANES, (hd + 1) * LANES)
        q = q_ref[:, sl].astype(F32)
        k = k_ref[:, sl].astype(F32)
        qo_ref[:, sl] = (rope(q) * QK_SCALE_LOG2).astype(BF16)
        ko_ref[:, sl] = rope(k).astype(BF16)
        vt_ref[0, hd, 0] = v_ref[:, sl].astype(F32).T.astype(BF16)


def _attn_prep(slab, cosf, s1, s2, bsz, s_len, tr):
    m = bsz * s_len
    nb = s_len // tr
    w = DIFF_QK_WIDTH
    row = pl.BlockSpec((tr, w), lambda i: (i, 0))
    tab = pl.BlockSpec((tr, LANES), lambda i: (i, 0))
    return pl.pallas_call(
        _attn_prep_kernel,
        grid=(m // tr,),
        in_specs=[
            pl.BlockSpec((tr, w), lambda i: (i, COL_DIFF_Q // w)),
            pl.BlockSpec((tr, w), lambda i: (i, COL_DIFF_K // w)),
            pl.BlockSpec((tr, w), lambda i: (i, COL_DIFF_V // w)),
            tab, tab, tab,
        ],
        out_specs=[
            row, row,
            pl.BlockSpec((1, DIFF_HEADS, 1, DIFF_V_DIM, tr), lambda i: (i // nb, 0, i % nb, 0, 0)),
        ],
        out_shape=[
            jax.ShapeDtypeStruct((m, w), BF16),
            jax.ShapeDtypeStruct((m, w), BF16),
            jax.ShapeDtypeStruct((bsz, DIFF_HEADS, nb, DIFF_V_DIM, tr), BF16),
        ],
        compiler_params=_cparams(("parallel",)),
        name="attn_prep",
    )(slab, slab, slab, cosf, s1, s2)


def _flash_kernel(q_ref, k_ref, vt_ref, lq1_ref, lk1_ref, lq2_ref, lk2_ref, ng_ref, o_ref,
                  sa_ref, sb_ref, *, tq, cw, lambda_init):
    qi = pl.program_id(2)
    nch = 2 * tq // cw
    lead = 1
    q = q_ref[...]
    lane = _iota(q.shape, 1)
    zero = jnp.zeros_like(q)
    qs = jnp.concatenate([jnp.where(lane < DIFF_HEAD_DIM, q, zero),
                          jnp.where(lane >= DIFF_HEAD_DIM, q, zero)], axis=0)
    q_chains = [qs[c * cw:(c + 1) * cw] for c in range(nch)]

    def keys(j):
        return k_ref[pl.ds(pl.multiple_of(j * tq, tq), tq), :]

    def scores(j, s_ref):
        kj = keys(j)
        for c in range(nch):
            s_ref[c] = _dot_nt(kj, q_chains[c])

    def absorb(j, s_ref, stats, masked, nxt_ref=None):
        vtj = vt_ref[0, 0, j]
        kn = None if nxt_ref is None else keys(j + 1)
        out = []
        if nxt_ref is not None:
            for c in range(min(lead, nch)):
                nxt_ref[c] = _dot_nt(kn, q_chains[c])
        for c in range(nch):
            if nxt_ref is not None and c + lead < nch:
                nxt_ref[c + lead] = _dot_nt(kn, q_chains[c + lead])
            m, l, acc = stats[c]
            if masked:
                q0 = (c * cw) % tq
                nk = q0 + cw
                kc = _iota((nk // CHUNK, 1, cw), 0)
                qc = (q0 + _iota((nk // CHUNK, 1, cw), 2)) // CHUNK
                bias = jnp.where(kc <= qc, 0.0, MASK_VALUE)
                s = (s_ref[c, 0:nk, :].reshape(nk // CHUNK, CHUNK, cw) + bias).reshape(nk, cw)
                vt = vtj[:, 0:nk]
            else:
                s = s_ref[c]
                vt = vtj
            m_new = jnp.maximum(m, jnp.max(s, axis=0, keepdims=True))
            p = jnp.exp2(s - m_new)
            alpha = jnp.exp2(m - m_new)
            l = alpha * l + jnp.sum(p, axis=0, keepdims=True)
            acc = acc * alpha + _dot(vt, p.astype(BF16))
            out.append((m_new, l, acc))
        return tuple(out)

    def pair(j, stats):
        stats = absorb(j, sa_ref, stats, False, sb_ref)
        return absorb(j + 1, sb_ref, stats, False, sa_ref)

    def quad(jj, stats):
        return pair(4 * jj + 2, pair(4 * jj, stats))

    def last_even(stats):
        return absorb(qi, sa_ref, stats, True)

    def last_odd(stats):
        stats = absorb(qi - 1, sa_ref, stats, False, sb_ref)
        return absorb(qi, sb_ref, stats, True)

    init = tuple((jnp.full((1, cw), MASK_VALUE, F32), jnp.zeros((1, cw), F32),
                  jnp.zeros((DIFF_V_DIM, cw), F32)) for _ in range(nch))
    scores(0, sa_ref)
    stats = lax.fori_loop(0, qi // 4, quad, init)
    stats = lax.cond(qi % 4 >= 2, lambda st: pair(4 * (qi // 4), st), lambda st: st, stats)
    carry = lax.cond(qi % 2 == 1, last_odd, last_even, stats)

    lam = (jnp.exp(jnp.sum(lq1_ref[...] * lk1_ref[...], keepdims=True))
           - jnp.exp(jnp.sum(lq2_ref[...] * lk2_ref[...], keepdims=True)) + lambda_init)
    on = jnp.concatenate([acc / l for (_, l, acc) in carry], axis=1)
    o = on[:, :tq] - lam * on[:, tq:]
    ms = jnp.mean(o * o, axis=0, keepdims=True)
    o = o * lax.rsqrt(ms + DIFF_SUBLN_EPS)
    o_ref[...] = (o.T * ng_ref[...] * (1.0 - lambda_init)).astype(o_ref.dtype)


def _flash(q_r, k_r, v_t, lq1, lk1, lq2, lk2, norm_g, bsz, s_len, tq, lambda_init):
    nq = s_len // tq
    cw = min(256, tq)
    vec = pl.BlockSpec((1, DIFF_HEAD_DIM), lambda b, h, i: (0, 0))
    return pl.pallas_call(
        functools.partial(_flash_kernel, tq=tq, cw=cw, lambda_init=lambda_init),
        grid=(bsz, DIFF_HEADS, nq),
        in_specs=[
            pl.BlockSpec((tq, LANES), lambda b, h, i: (b * nq + i, h)),
            pl.BlockSpec((s_len, LANES), lambda b, h, i: (b, h)),
            pl.BlockSpec((1, 1, nq, DIFF_V_DIM, tq), lambda b, h, i: (b, h, 0, 0, 0)),
            vec, vec, vec, vec,
            pl.BlockSpec((1, DIFF_V_DIM), lambda b, h, i: (0, 0)),
        ],
        out_specs=pl.BlockSpec((tq, DIFF_V_DIM), lambda b, h, i: (b * nq + i, h)),
        out_shape=jax.ShapeDtypeStruct((bsz * s_len, DIFF_V_WIDTH), BF16),
        scratch_shapes=[pltpu.VMEM((2 * tq // cw, tq, cw), F32),
                        pltpu.VMEM((2 * tq // cw, tq, cw), F32)],
        compiler_params=_cparams(("parallel", "parallel", "arbitrary")),
        name="flash",
    )(q_r, k_r, v_t, lq1, lk1, lq2, lk2, norm_g)


def _merge_kernel(x_ref, yg_ref, ys_ref, yd_ref, g0_ref, g1_ref, g2_ref, bg_ref,
                  wg_ref, ws_ref, wd_ref, wo_ref, o_ref):
    def gate(g_ref, i):
        return 1.0 / (1.0 + jnp.exp(-(g_ref[...].astype(F32) + bg_ref[i:i + 1, :])))

    mixed = gate(g0_ref, 0) * _dot(yg_ref[...], wg_ref[...])
    mixed = mixed + gate(g1_ref, 1) * _dot(ys_ref[...], ws_ref[...])
    mixed = mixed + gate(g2_ref, 2) * _dot(yd_ref[...], wd_ref[...])
    o_ref[...] = x_ref[...] + _dot(mixed.astype(BF16), wo_ref[...])


def _merge(x2, y_gla, y_ssm, y_diff, slab, b_gate, w_g, w_s, w_d, w_o, layer, tm=512):
    m = x2.shape[0]
    gcol = COL_GATE // D_MODEL
    row = lambda w: pl.BlockSpec((tm, w), lambda i: (i, 0))
    const = lambda r, c: pl.BlockSpec((r, c), lambda i: (0, 0))
    weight = lambda r, c: pl.BlockSpec((None, r, c), lambda i: (layer, 0, 0))
    return pl.pallas_call(
        _merge_kernel,
        grid=(m // tm,),
        in_specs=[
            row(D_MODEL), row(GLA_VAL_DIM), row(SSM_D_INNER), row(DIFF_V_WIDTH),
            pl.BlockSpec((tm, D_MODEL), lambda i: (i, gcol)),
            pl.BlockSpec((tm, D_MODEL), lambda i: (i, gcol + 1)),
            pl.BlockSpec((tm, D_MODEL), lambda i: (i, gcol + 2)),
            const(N_BRANCHES, D_MODEL),
            weight(GLA_VAL_DIM, D_MODEL), weight(SSM_D_INNER, D_MODEL), weight(DIFF_V_WIDTH, D_MODEL),
            weight(D_MODEL, D_MODEL),
        ],
        out_specs=row(D_MODEL),
        out_shape=jax.ShapeDtypeStruct((m, D_MODEL), F32),
        compiler_params=_cparams(("parallel",)),
        name="merge",
    )(x2, y_gla, y_ssm, y_diff, slab, slab, slab, b_gate, w_g, w_s, w_d, w_o)


def _mlp_kernel(x_ref, g_ref, wu_ref, wd_ref, gf_ref, o_ref, *, final_norm, fc):
    x = x_ref[...]
    ms = jnp.mean(x * x, axis=-1, keepdims=True)
    h = (x * lax.rsqrt(ms + EPS) * g_ref[...]).astype(BF16)
    acc = x
    for c0 in range(0, D_FF, fc):
        u = jnp.maximum(_dot(h, wu_ref[:, c0:c0 + fc]), 0.0)
        acc = acc + _dot((u * u).astype(BF16), wd_ref[c0:c0 + fc, :])
    if final_norm:
        ms = jnp.mean(acc * acc, axis=-1, keepdims=True)
        acc = acc * lax.rsqrt(ms + EPS) * gf_ref[...]
    o_ref[...] = acc


def _mlp(x2, g, w_up, w_down, g_final, final_norm, layer, tm=512, fc=1024):
    m = x2.shape[0]
    row = pl.BlockSpec((tm, D_MODEL), lambda i: (i, 0))
    const = lambda r, c: pl.BlockSpec((r, c), lambda i: (0, 0))
    weight = lambda r, c: pl.BlockSpec((None, r, c), lambda i: (layer, 0, 0))
    return pl.pallas_call(
        functools.partial(_mlp_kernel, final_norm=final_norm, fc=fc),
        grid=(m // tm,),
        in_specs=[row, const(1, D_MODEL), weight(D_MODEL, D_FF), weight(D_FF, D_MODEL), const(1, D_MODEL)],
        out_specs=row,
        out_shape=jax.ShapeDtypeStruct((m, D_MODEL), F32),
        compiler_params=_cparams(("parallel",)),
        name="mlp",
    )(x2, g, w_up, w_down, g_final)


def _slab_weights(w_in):
    sizes = (GLA_KEY_DIM, GLA_KEY_DIM, GLA_VAL_DIM, GLA_LOW_RANK, GLA_VAL_DIM,
             SSM_D_INNER, SSM_CONV_DIM, SSM_HEADS,
             DIFF_QK_WIDTH, DIFF_QK_WIDTH, DIFF_V_WIDTH, N_BRANCHES * D_MODEL)
    offs = np.concatenate([[0], np.cumsum(sizes)])
    (a_q, a_k, a_v, a_gk, a_g, b_z, b_xbc, b_dt, c_q, c_k, c_v, gate) = [
        w_in[:, :, int(offs[i]):int(offs[i + 1])] for i in range(len(sizes))]
    slab = jnp.concatenate([b_xbc, gate, a_v, a_g, c_q, c_k, c_v, a_q, a_k, b_z], axis=2).astype(BF16)
    zeros = lambda n: jnp.zeros((w_in.shape[0], D_MODEL, n), w_in.dtype)
    small = jnp.concatenate([a_gk, zeros(LANES - GLA_LOW_RANK), b_dt, zeros(LANES - SSM_HEADS)],
                            axis=2).astype(BF16)
    return slab, small


def _pad_lanes(v, fill=0.0):
    v = v.reshape(1, -1).astype(F32)
    return jnp.pad(v, ((0, 0), (0, LANES - v.shape[1])), constant_values=fill)


def _forward(x, positions, norm_mix_g, w_in, b_gate, gla_w_gk2, gla_b_gk, gla_norm_g,
             ssm_conv_w, ssm_conv_b, ssm_dt_bias, ssm_a_log, ssm_d, ssm_norm_g,
             diff_lq1, diff_lk1, diff_lq2, diff_lk2, diff_norm_g,
             w_br_gla, w_br_ssm, w_br_diff, w_out, norm_mlp_g, w_mlp_up, w_mlp_down,
             norm_final_g, tq):
    bsz, s_len, _ = x.shape
    m = bsz * s_len
    depth = w_in.shape[0]
    x2 = x.reshape(m, D_MODEL).astype(F32)

    head_of_lane = np.arange(SSM_D_INNER) // SSM_HEAD_DIM
    expand = jnp.asarray((np.arange(LANES)[:, None] == head_of_lane[None, :]), BF16)
    pairsel = jnp.asarray((np.arange(LANES)[None, :] // 2) == np.arange(SSM_HEADS // 2)[:, None], BF16)
    pairsel = pairsel * jnp.asarray(np.arange(LANES)[None, :] < SSM_HEADS, BF16)
    t = np.arange((SSM_CONV_W - 1) * CHUNK)
    src_row = SSM_CONV_PAD + (t % CHUNK) - (SSM_CONV_W - 1) + (t // CHUNK)
    shift = jnp.asarray(np.arange(CHUNK + SSM_CONV_PAD)[None, :] == src_row[:, None], BF16)
    d = np.arange(LANES) % DIFF_HEAD_DIM
    inv_freq = ROPE_THETA ** (-(2.0 * (d % (ROT_DIM // 2))) / ROT_DIM)
    freq_row = jnp.asarray(np.where(d < ROT_DIM, inv_freq, 0.0).reshape(1, LANES), F32)

    cosf, s1, s2 = _rope_tables(positions.reshape(m, 1), freq_row, tr=min(512, s_len))

    w_slab, w_small = _slab_weights(w_in)
    wb_gla, wb_ssm, wb_diff, wb_out = (w.astype(BF16) for w in (w_br_gla, w_br_ssm, w_br_diff, w_out))
    wb_up, wb_down = w_mlp_up.astype(BF16), w_mlp_down.astype(BF16)

    for l in range(depth):
        slab, small = _inproj(x2, norm_mix_g[l].reshape(1, -1), w_slab, w_small, l,
                              tm=min(1024, m))

        wgk_pad = jnp.pad(gla_w_gk2[l].astype(F32), ((0, LANES - GLA_LOW_RANK), (0, 0)))
        b_all, dt, a_cum = _gates(small, wgk_pad, gla_b_gk[l].reshape(1, -1),
                                  _pad_lanes(ssm_dt_bias[l]), _pad_lanes(ssm_a_log[l]))
        y_gla = _gla(slab, b_all, gla_norm_g[l].reshape(1, -1), bsz, s_len)

        d_exp = jnp.repeat(ssm_d[l].astype(F32), SSM_HEAD_DIM).reshape(1, -1)
        y_ssm = _ssd(slab, dt, a_cum, ssm_conv_w[l], ssm_conv_b[l].reshape(1, -1), d_exp,
                     ssm_norm_g[l].reshape(1, -1), expand, pairsel, shift, bsz, s_len)

        lambda_init = 0.8 - 0.6 * math.exp(-0.3 * l)
        q_r, k_r, v_t = _attn_prep(slab, cosf, s1, s2, bsz, s_len, tq)
        y_diff = _flash(q_r, k_r, v_t, diff_lq1[l].reshape(1, -1), diff_lk1[l].reshape(1, -1),
                        diff_lq2[l].reshape(1, -1), diff_lk2[l].reshape(1, -1),
                        diff_norm_g[l].reshape(1, -1), bsz, s_len, tq, lambda_init)

        x2 = _merge(x2, y_gla, y_ssm, y_diff, slab, b_gate[l].reshape(N_BRANCHES, D_MODEL),
                    wb_gla, wb_ssm, wb_diff, wb_out, l, tm=min(512, m))
        x2 = _mlp(x2, norm_mlp_g[l].reshape(1, -1), wb_up, wb_down, norm_final_g.reshape(1, -1),
                  final_norm=(l == depth - 1), layer=l, tm=min(512, m))
    return x2.reshape(bsz, s_len, D_MODEL)


def kernel(x, positions, norm_mix_g, w_in, b_gate, gla_w_gk2, gla_b_gk, gla_norm_g, ssm_conv_w,
           ssm_conv_b, ssm_dt_bias, ssm_a_log, ssm_d, ssm_norm_g, diff_lq1, diff_lk1, diff_lq2,
           diff_lk2, diff_norm_g, w_br_gla, w_br_ssm, w_br_diff, w_out, norm_mlp_g, w_mlp_up,
           w_mlp_down, norm_final_g):
    return _forward(x, positions, norm_mix_g, w_in, b_gate, gla_w_gk2, gla_b_gk, gla_norm_g,
                    ssm_conv_w, ssm_conv_b, ssm_dt_bias, ssm_a_log, ssm_d, ssm_norm_g,
                    diff_lq1, diff_lk1, diff_lq2, diff_lk2, diff_norm_g,
                    w_br_gla, w_br_ssm, w_br_diff, w_out, norm_mlp_g, w_mlp_up, w_mlp_down,
                    norm_final_g, tq=min(512, x.shape[1]))
```

```python
import functools
import math

import jax
import jax.numpy as jnp
import numpy as np
from jax import lax
from jax.experimental import pallas as pl
from jax.experimental.pallas import tpu as pltpu

F32 = jnp.float32
BF16 = jnp.bfloat16

D_MODEL = 1024
DEPTH = 2
CHUNK = 64
ROPE_THETA = 500000.0
EPS = 1e-6

GLA_HEADS = 4
GLA_KEY_DIM = D_MODEL // 2
GLA_VAL_DIM = D_MODEL
GLA_DK = GLA_KEY_DIM // GLA_HEADS
GLA_DV = GLA_VAL_DIM // GLA_HEADS
GLA_LOW_RANK = 16
GLA_GATE_NORMALIZER = 16.0

SSM_D_INNER = 2 * D_MODEL
SSM_HEAD_DIM = 64
SSM_HEADS = SSM_D_INNER // SSM_HEAD_DIM
SSM_GROUPS = 4
SSM_HEADS_PER_GROUP = SSM_HEADS // SSM_GROUPS
SSM_STATE = 128
SSM_CONV_W = 4
SSM_BC_WIDTH = SSM_GROUPS * SSM_STATE
SSM_CONV_DIM = SSM_D_INNER + 2 * SSM_BC_WIDTH
SSM_GROUP_WIDTH = SSM_D_INNER // SSM_GROUPS
SSM_CONV_PAD = 16
RECUR_CHUNKS_PER_STEP = 4

DIFF_HEADS = 8
DIFF_HEAD_DIM = 64
DIFF_V_DIM = 2 * DIFF_HEAD_DIM
DIFF_QK_WIDTH = DIFF_HEADS * 2 * DIFF_HEAD_DIM
DIFF_V_WIDTH = DIFF_HEADS * DIFF_V_DIM
ROT_DIM = DIFF_HEAD_DIM // 4
DIFF_SUBLN_EPS = 1e-5

N_BRANCHES = 3
D_FF = 4 * D_MODEL

LANES = 128
MASK_VALUE = -1e30
QK_SCALE_LOG2 = DIFF_HEAD_DIM ** -0.5 * math.log2(math.e)

COL_XBC = 0
COL_GATE = COL_XBC + SSM_CONV_DIM
COL_GLA_V = COL_GATE + N_BRANCHES * D_MODEL
COL_GLA_G = COL_GLA_V + GLA_VAL_DIM
COL_DIFF_Q = COL_GLA_G + GLA_VAL_DIM
COL_DIFF_K = COL_DIFF_Q + DIFF_QK_WIDTH
COL_DIFF_V = COL_DIFF_K + DIFF_QK_WIDTH
COL_GLA_Q = COL_DIFF_V + DIFF_V_WIDTH
COL_GLA_K = COL_GLA_Q + GLA_KEY_DIM
COL_SSM_Z = COL_GLA_K + GLA_KEY_DIM
SLAB_COLS = COL_SSM_Z + SSM_D_INNER
SMALL_COLS = 2 * LANES

VMEM_LIMIT = 56 * 1024 * 1024


def _cparams(sem):
    return pltpu.CompilerParams(dimension_semantics=sem, vmem_limit_bytes=VMEM_LIMIT)


def _dot(a, b):
    return jnp.dot(a, b, preferred_element_type=F32)


def _dot_nt(a, b):
    return lax.dot_general(a, b, (((1,), (1,)), ((), ())), preferred_element_type=F32)


def _split3(x):
    hi = x.astype(BF16)
    r1 = x - hi.astype(F32)
    mid = r1.astype(BF16)
    lo = (r1 - mid.astype(F32)).astype(BF16)
    return hi, mid, lo


def _split2(x):
    hi = x.astype(BF16)
    return hi, (x - hi.astype(F32)).astype(BF16)


def _dot_sel_lhs2(sel, x):
    hi, mid = _split2(x)
    return _dot(sel, hi) + _dot(sel, mid)


def _dot_sel_rhs2(x, sel):
    hi, mid = _split2(x)
    return _dot(hi, sel) + _dot(mid, sel)


def _dot_nt_sel_lhs2(sel, x):
    hi, mid = _split2(x)
    return _dot_nt(sel, hi) + _dot_nt(sel, mid)


def _dot_f32(a, b):
    ah, am, _ = _split3(a)
    bh, bm, _ = _split3(b)
    return _dot(ah, bh) + _dot(ah, bm) + _dot(am, bh)


def _silu(x):
    return x / (1.0 + jnp.exp(-x))


def _log1p_exp_neg_abs(x):
    return jnp.log(1.0 + jnp.exp(-jnp.abs(x)))


def _log_sigmoid(x):
    return jnp.minimum(x, 0.0) - _log1p_exp_neg_abs(x)


def _softplus(x):
    return jnp.maximum(x, 0.0) + _log1p_exp_neg_abs(x)


def _iota(shape, dim):
    return lax.broadcasted_iota(jnp.int32, shape, dim)


def _inproj_kernel(x_ref, g_ref, w_ref, ws_ref, o_ref, os_ref, h_ref):
    @pl.when(pl.program_id(1) == 0)
    def _():
        x = x_ref[...]
        ms = jnp.mean(x * x, axis=-1, keepdims=True)
        h = (x * lax.rsqrt(ms + EPS) * g_ref[...]).astype(BF16)
        h_ref[...] = h
        os_ref[...] = _dot(h, ws_ref[...])

    o_ref[...] = _dot(h_ref[...], w_ref[...]).astype(o_ref.dtype)


def _inproj(x2, g, w_slab, w_small, layer, tm=1024, tn=2048):
    m = x2.shape[0]
    return pl.pallas_call(
        _inproj_kernel,
        grid=(m // tm, SLAB_COLS // tn),
        in_specs=[
            pl.BlockSpec((tm, D_MODEL), lambda i, j: (i, 0)),
            pl.BlockSpec((1, D_MODEL), lambda i, j: (0, 0)),
            pl.BlockSpec((None, D_MODEL, tn), lambda i, j: (layer, 0, j)),
            pl.BlockSpec((None, D_MODEL, SMALL_COLS), lambda i, j: (layer, 0, 0)),
        ],
        out_specs=[
            pl.BlockSpec((tm, tn), lambda i, j: (i, j)),
            pl.BlockSpec((tm, SMALL_COLS), lambda i, j: (i, 0)),
        ],
        out_shape=[
            jax.ShapeDtypeStruct((m, SLAB_COLS), BF16),
            jax.ShapeDtypeStruct((m, SMALL_COLS), F32),
        ],
        scratch_shapes=[pltpu.VMEM((tm, D_MODEL), BF16)],
        compiler_params=_cparams(("parallel", "arbitrary")),
        name="inproj",
    )(x2, g, w_slab, w_small)


def _gates_kernel(sm_ref, wgk_ref, bgk_ref, dtb_ref, alog_ref, b_ref, dt_ref, acum_ref):
    rows = sm_ref.shape[0]
    r = _iota((rows, rows), 0)
    c = _iota((rows, rows), 1)
    tril = jnp.where((c <= r) & ((r // CHUNK) == (c // CHUNK)), 1.0, 0.0).astype(BF16)

    gk = _log_sigmoid(_dot_f32(sm_ref[:, :LANES], wgk_ref[...]) + bgk_ref[...]) / GLA_GATE_NORMALIZER
    b_ref[...] = _dot_sel_lhs2(tril, gk)

    dt = _softplus(sm_ref[:, LANES:] + dtb_ref[...])
    dt_ref[...] = dt
    acum_ref[...] = _dot_sel_lhs2(tril, dt * (-jnp.exp(alog_ref[...])))


def _gates(small, wgk_pad, b_gk, dtb_pad, alog_pad, rows=256):
    m = small.shape[0]
    rows = min(rows, m)
    const = lambda r, c: pl.BlockSpec((r, c), lambda i: (0, 0))
    out = lambda w: pl.BlockSpec((rows, w), lambda i: (i, 0))
    return pl.pallas_call(
        _gates_kernel,
        grid=(m // rows,),
        in_specs=[pl.BlockSpec((rows, SMALL_COLS), lambda i: (i, 0)),
                  const(LANES, GLA_KEY_DIM), const(1, GLA_KEY_DIM), const(1, LANES), const(1, LANES)],
        out_specs=[out(GLA_KEY_DIM), out(LANES), out(LANES)],
        out_shape=[jax.ShapeDtypeStruct((m, GLA_KEY_DIM), F32),
                   jax.ShapeDtypeStruct((m, LANES), F32),
                   jax.ShapeDtypeStruct((m, LANES), F32)],
        compiler_params=_cparams(("parallel",)),
        name="gates",
    )(small, wgk_pad, b_gk, dtb_pad, alog_pad)


def _level_anchor(b, h):
    c = CHUNK
    if 2 * h >= 8:
        parts = [jnp.broadcast_to(b[s + h - 1:s + h, :], (2 * h, b.shape[1]))
                 for s in range(0, c, 2 * h)]
        return parts[0] if len(parts) == 1 else jnp.concatenate(parts, axis=0)
    pos = _iota(b.shape, 0) % (2 * h)
    if h == 1:
        return jnp.where(pos == 1, pltpu.roll(b, 1, 0), b)
    return jnp.where(pos == 0, pltpu.roll(b, c - 1, 0),
                     jnp.where(pos == 1, b,
                               jnp.where(pos == 2, pltpu.roll(b, 1, 0), pltpu.roll(b, 2, 0))))


def _gla_kernel(q_ref, k_ref, v_ref, g_ref, b_ref, ng_ref, o_ref, st_ref):
    @pl.when(pl.program_id(1) == 0)
    def _():
        st_ref[...] = jnp.zeros_like(st_ref)

    for ci in range(q_ref.shape[0] // CHUNK):
        rows = slice(ci * CHUNK, (ci + 1) * CHUNK)
        _gla_chunk(q_ref.at[rows], k_ref.at[rows], v_ref.at[rows], g_ref.at[rows], b_ref.at[rows],
                   ng_ref, o_ref.at[rows], st_ref)


def _gla_chunk(q_ref, k_ref, v_ref, g_ref, b_ref, ng_ref, o_ref, st_ref):
    c = CHUNK
    b_all = b_ref[...]
    row = _iota((c, c), 0)
    col = _iota((c, c), 1)
    eye = row == col

    heads = range(GLA_HEADS)
    ksl = [slice(hd * GLA_DK, (hd + 1) * GLA_DK) for hd in heads]
    vsl = [slice(hd * GLA_DV, (hd + 1) * GLA_DV) for hd in heads]
    q = q_ref[...].astype(F32) * (GLA_DK ** -0.5)
    k = k_ref[...].astype(F32)
    b = b_all
    b_last = b[c - 1:c, :]
    st = [st_ref[hd] for hd in heads]

    qg = (q * jnp.exp(b)).astype(BF16)
    kd = (k * jnp.exp(b_last - b)).astype(BF16)
    e_last = jnp.exp(b_last)
    o = [_dot_nt(qg[:, ksl[hd]], st[hd].astype(BF16)) for hd in heads]
    for hd in heads:
        vt = v_ref[:, vsl[hd]].astype(F32).T.astype(BF16)
        st_ref[hd] = st[hd] * e_last[:, ksl[hd]] + _dot(vt, kd[:, ksl[hd]])

    q16 = q.astype(BF16)
    k16 = k.astype(BF16)
    scores = [jnp.where(eye, _dot_nt(q16[:, ksl[hd]], k16[:, ksl[hd]]), 0.0) for hd in heads]
    h = c // 2
    while h >= 1:
        anchor = _level_anchor(b, h)
        second = (_iota(b.shape, 0) % (2 * h)) >= h
        decay = jnp.exp(jnp.where(second, b - anchor, anchor - b))
        qt = jnp.where(second, q * decay, 0.0).astype(BF16)
        kt = jnp.where(second, 0.0, k * decay).astype(BF16)
        same_block = (row // (2 * h)) == (col // (2 * h))
        for hd in heads:
            p = _dot_nt(qt[:, ksl[hd]], kt[:, ksl[hd]])
            if 2 * h < c:
                p = jnp.where(same_block, p, 0.0)
            scores[hd] = scores[hd] + p
        h //= 2

    for hd in heads:
        oh = o[hd] + _dot(scores[hd].astype(BF16), v_ref[:, vsl[hd]])
        g = g_ref[:, vsl[hd]].astype(F32)
        ms = jnp.mean(oh * oh, axis=-1, keepdims=True)
        o_ref[:, vsl[hd]] = (oh * lax.rsqrt(ms + EPS) * ng_ref[...] * _silu(g)).astype(o_ref.dtype)


def _gla(slab, b_all, norm_g, bsz, s_len):
    tr = RECUR_CHUNKS_PER_STEP * CHUNK
    nc = s_len // tr
    rb = lambda b, i: b * nc + i
    return pl.pallas_call(
        _gla_kernel,
        grid=(bsz, nc),
        in_specs=[
            pl.BlockSpec((tr, GLA_KEY_DIM), lambda b, i: (rb(b, i), COL_GLA_Q // GLA_KEY_DIM)),
            pl.BlockSpec((tr, GLA_KEY_DIM), lambda b, i: (rb(b, i), COL_GLA_K // GLA_KEY_DIM)),
            pl.BlockSpec((tr, GLA_VAL_DIM), lambda b, i: (rb(b, i), COL_GLA_V // GLA_VAL_DIM)),
            pl.BlockSpec((tr, GLA_VAL_DIM), lambda b, i: (rb(b, i), COL_GLA_G // GLA_VAL_DIM)),
            pl.BlockSpec((tr, GLA_KEY_DIM), lambda b, i: (rb(b, i), 0)),
            pl.BlockSpec((1, GLA_DV), lambda b, i: (0, 0)),
        ],
        out_specs=pl.BlockSpec((tr, GLA_VAL_DIM), lambda b, i: (rb(b, i), 0)),
        out_shape=jax.ShapeDtypeStruct((bsz * s_len, GLA_VAL_DIM), BF16),
        scratch_shapes=[pltpu.VMEM((GLA_HEADS, GLA_DV, GLA_DK), F32)],
        compiler_params=_cparams(("parallel", "arbitrary")),
        name="gla",
    )(slab, slab, slab, slab, b_all, norm_g)


def _ssd_kernel(xbc_ref, z_ref, dt_ref, acum_ref, cw_ref, cb_ref, dexp_ref, ng_ref,
                expand_ref, pairsel_ref, shift_ref, o_ref, xpad_ref, st_ref, yd_ref):
    pad = SSM_CONV_PAD
    tr = xbc_ref.shape[0]

    @pl.when(pl.program_id(1) == 0)
    def _():
        st_ref[...] = jnp.zeros_like(st_ref)
        xpad_ref[0:pad, :] = jnp.zeros((pad, SSM_CONV_DIM), BF16)

    xpad_ref[pad:pad + tr, :] = xbc_ref[...]
    for ci in range(tr // CHUNK):
        rows = slice(ci * CHUNK, (ci + 1) * CHUNK)
        _ssd_chunk(xpad_ref.at[ci * CHUNK:(ci + 1) * CHUNK + pad], xbc_ref.at[rows], z_ref.at[rows],
                   dt_ref.at[rows], acum_ref.at[rows], cw_ref, cb_ref, dexp_ref, ng_ref,
                   expand_ref, pairsel_ref, shift_ref, o_ref.at[rows], st_ref, yd_ref.at[rows])
    xpad_ref[0:pad, :] = xpad_ref[tr:tr + pad, :]


def _ssd_chunk(xwin_ref, xbc_ref, z_ref, dt_ref, acum_ref, cw_ref, cb_ref, dexp_ref, ng_ref,
               expand_ref, pairsel_ref, shift_ref, o_ref, st_ref, yd_ref):
    c = CHUNK
    n = SSM_STATE
    gw = SSM_GROUP_WIDTH

    delayed = _dot(shift_ref[...], xwin_ref[...])
    conv = cb_ref[...] + xbc_ref[...].astype(F32) * cw_ref[SSM_CONV_W - 1:SSM_CONV_W, :]
    for i in range(SSM_CONV_W - 1):
        conv = conv + delayed[i * c:(i + 1) * c] * cw_ref[i:i + 1, :]
    xc = _silu(conv)
    xs = xc[:, :SSM_D_INNER]

    a_cum = acum_ref[...]
    both = _dot_sel_rhs2(jnp.concatenate([dt_ref[...], a_cum], axis=0), expand_ref[...])
    dt_e = both[:c]
    a_e = both[c:]
    a_last_e = a_e[c - 1:c, :]
    xdt = xs * dt_e
    decay_out = jnp.exp(a_e)
    xdtd = (xdt * jnp.exp(a_last_e - a_e)).astype(BF16)
    chunk_decay = jnp.exp(a_last_e)

    lane = _iota((c, LANES), 1)
    z0 = jnp.concatenate([jnp.where(lane % 2 == 0, a_cum, 0.0),
                          jnp.where(lane % 2 == 1, a_cum, 0.0)], axis=0)
    pairs = _dot_nt_sel_lhs2(pairsel_ref[...], z0)

    lane2 = _iota((c, 2 * c), 1)
    row2 = _iota((c, 2 * c), 0)
    causal2 = (lane2 % c) <= row2
    bd_row = _iota((2 * c, 2 * c), 0)
    bd_col = _iota((2 * c, 2 * c), 1)
    bd_mask = (bd_row // c) == (bd_col // c)

    for g in range(SSM_GROUPS):
        bg = xc[:, SSM_D_INNER + g * n:SSM_D_INNER + (g + 1) * n]
        cg = xc[:, SSM_D_INNER + SSM_BC_WIDTH + g * n:SSM_D_INNER + SSM_BC_WIDTH + (g + 1) * n]
        cg16 = cg.astype(BF16)
        bg16 = bg.astype(BF16)
        gsl = slice(g * gw, (g + 1) * gw)
        st = st_ref[g]

        y_off = _dot(cg16, st.astype(BF16)) * decay_out[:, gsl]
        st_ref[g] = st * chunk_decay[:, gsl] + _dot(bg.T.astype(BF16), xdtd[:, gsl])

        cb2 = _dot_nt(cg16, jnp.concatenate([bg16, bg16], axis=0))
        for pr in range(SSM_HEADS_PER_GROUP // 2):
            pidx = g * (SSM_HEADS_PER_GROUP // 2) + pr
            lsl = slice(pidx * 2 * c, (pidx + 1) * 2 * c)
            seg = a_e[:, lsl] - pairs[pidx:pidx + 1, :]
            lmat = jnp.exp(jnp.where(causal2, seg, MASK_VALUE))
            w = (cb2 * lmat).astype(BF16)
            xp = xdt[:, lsl]
            bd = jnp.where(bd_mask, jnp.concatenate([xp, xp], axis=0), 0.0).astype(BF16)
            yd_ref[:, lsl] = _dot(w, bd)

        y = yd_ref[:, gsl] + y_off + xs[:, gsl] * dexp_ref[:, gsl]
        y = y * _silu(z_ref[:, gsl].astype(F32))
        ms = jnp.mean(y * y, axis=-1, keepdims=True)
        o_ref[:, gsl] = (y * lax.rsqrt(ms + EPS) * ng_ref[:, gsl]).astype(o_ref.dtype)


def _ssd(slab, dt, a_cum, conv_w, conv_b, d_exp, norm_g, expand, pairsel, shift, bsz, s_len):
    tr = RECUR_CHUNKS_PER_STEP * CHUNK
    nc = s_len // tr
    rb = lambda b, i: b * nc + i
    const = lambda b, i: (0, 0)
    return pl.pallas_call(
        _ssd_kernel,
        grid=(bsz, nc),
        in_specs=[
            pl.BlockSpec((tr, SSM_CONV_DIM), lambda b, i: (rb(b, i), COL_XBC // SSM_CONV_DIM)),
            pl.BlockSpec((tr, SSM_D_INNER), lambda b, i: (rb(b, i), COL_SSM_Z // SSM_D_INNER)),
            pl.BlockSpec((tr, LANES), lambda b, i: (rb(b, i), 0)),
            pl.BlockSpec((tr, LANES), lambda b, i: (rb(b, i), 0)),
            pl.BlockSpec((SSM_CONV_W, SSM_CONV_DIM), const),
            pl.BlockSpec((1, SSM_CONV_DIM), const),
            pl.BlockSpec((1, SSM_D_INNER), const),
            pl.BlockSpec((1, SSM_D_INNER), const),
            pl.BlockSpec((LANES, SSM_D_INNER), const),
            pl.BlockSpec((SSM_HEADS // 2, LANES), const),
            pl.BlockSpec(((SSM_CONV_W - 1) * CHUNK, CHUNK + SSM_CONV_PAD), const),
        ],
        out_specs=pl.BlockSpec((tr, SSM_D_INNER), lambda b, i: (rb(b, i), 0)),
        out_shape=jax.ShapeDtypeStruct((bsz * s_len, SSM_D_INNER), BF16),
        scratch_shapes=[
            pltpu.VMEM((tr + SSM_CONV_PAD, SSM_CONV_DIM), BF16),
            pltpu.VMEM((SSM_GROUPS, SSM_STATE, SSM_GROUP_WIDTH), F32),
            pltpu.VMEM((tr, SSM_D_INNER), F32),
        ],
        compiler_params=_cparams(("parallel", "arbitrary")),
        name="ssd",
    )(slab, slab, dt, a_cum, conv_w, conv_b, d_exp, norm_g, expand, pairsel, shift)


def _rope_table_kernel(pos_ref, freq_ref, c_ref, s1_ref, s2_ref):
    ang = pos_ref[...].astype(F32) * freq_ref[...]
    d = _iota(ang.shape, 1) % DIFF_HEAD_DIM
    half = ROT_DIM // 2
    cosv = jnp.cos(ang)
    sinv = jnp.sin(ang)
    c_ref[...] = jnp.where(d < ROT_DIM, cosv, 1.0)
    s1_ref[...] = jnp.where(d < half, -sinv, 0.0)
    s2_ref[...] = jnp.where((d >= half) & (d < ROT_DIM), sinv, 0.0)


def _rope_tables(pos_col, freq_row, tr=512):
    m = pos_col.shape[0]
    spec = pl.BlockSpec((tr, LANES), lambda i: (i, 0))
    shp = jax.ShapeDtypeStruct((m, LANES), F32)
    return pl.pallas_call(
        _rope_table_kernel,
        grid=(m // tr,),
        in_specs=[pl.BlockSpec((tr, 1), lambda i: (i, 0)), pl.BlockSpec((1, LANES), lambda i: (0, 0))],
        out_specs=[spec, spec, spec],
        out_shape=[shp, shp, shp],
        compiler_params=_cparams(("parallel",)),
        name="rope_tables",
    )(pos_col, freq_row)


def _attn_prep_kernel(q_ref, k_ref, v_ref, c_ref, s1_ref, s2_ref, qo_ref, ko_ref, vt_ref):
    half = ROT_DIM // 2
    cosf = c_ref[...]
    s1 = s1_ref[...]
    s2 = s2_ref[...]

    def rope(t):
        return t * cosf + pltpu.roll(t, LANES - half, 1) * s1 + pltpu.roll(t, half, 1) * s2

    for hd in range(DIFF_HEADS):
        sl = slice(hd * LANES, (hd + 1) * LANES)
        q = q_ref[:, sl].astype(F32)
        k = k_ref[:, sl].astype(F32)
        qo_ref[:, sl] = (rope(q) * QK_SCALE_LOG2).astype(BF16)
        ko_ref[:, sl] = rope(k).astype(BF16)
        vt_ref[0, hd, 0] = v_ref[:, sl].astype(F32).T.astype(BF16)


def _attn_prep(slab, cosf, s1, s2, bsz, s_len, tr):
    m = bsz * s_len
    nb = s_len // tr
    w = DIFF_QK_WIDTH
    row = pl.BlockSpec((tr, w), lambda i: (i, 0))
    tab = pl.BlockSpec((tr, LANES), lambda i: (i, 0))
    return pl.pallas_call(
        _attn_prep_kernel,
        grid=(m // tr,),
        in_specs=[
            pl.BlockSpec((tr, w), lambda i: (i, COL_DIFF_Q // w)),
            pl.BlockSpec((tr, w), lambda i: (i, COL_DIFF_K // w)),
            pl.BlockSpec((tr, w), lambda i: (i, COL_DIFF_V // w)),
            tab, tab, tab,
        ],
        out_specs=[
            row, row,
            pl.BlockSpec((1, DIFF_HEADS, 1, DIFF_V_DIM, tr), lambda i: (i // nb, 0, i % nb, 0, 0)),
        ],
        out_shape=[
            jax.ShapeDtypeStruct((m, w), BF16),
            jax.ShapeDtypeStruct((m, w), BF16),
            jax.ShapeDtypeStruct((bsz, DIFF_HEADS, nb, DIFF_V_DIM, tr), BF16),
        ],
        compiler_params=_cparams(("parallel",)),
        name="attn_prep",
    )(slab, slab, slab, cosf, s1, s2)


def _flash_kernel(q_ref, qn_ref, k_ref, vt_ref, lq1_ref, lk1_ref, lq2_ref, lk2_ref, ng_ref, o_ref,
                  sa_ref, sb_ref, sc_ref, *, tq, cw, lambda_init):
    qi = pl.program_id(2)
    nch = 2 * tq // cw
    lead = 1

    def chains(ref):
        q = ref[...]
        lane = _iota(q.shape, 1)
        zero = jnp.zeros_like(q)
        qs = jnp.concatenate([jnp.where(lane < DIFF_HEAD_DIM, q, zero),
                              jnp.where(lane >= DIFF_HEAD_DIM, q, zero)], axis=0)
        return [qs[c * cw:(c + 1) * cw] for c in range(nch)]

    q_chains = chains(q_ref)

    def keys(j):
        return k_ref[pl.ds(pl.multiple_of(j * tq, tq), tq), :]

    def scores(j, s_ref, qc):
        kj = keys(j)
        for c in range(nch):
            s_ref[c] = _dot_nt(kj, qc[c])

    def absorb(j, s_ref, stats, masked, nxt_ref=None):
        vtj = vt_ref[0, 0, j]
        kn = None if nxt_ref is None else keys(j + 1)
        out = []
        if nxt_ref is not None:
            for c in range(min(lead, nch)):
                nxt_ref[c] = _dot_nt(kn, q_chains[c])
        for c in range(nch):
            if nxt_ref is not None and c + lead < nch:
                nxt_ref[c + lead] = _dot_nt(kn, q_chains[c + lead])
            m, l, acc = stats[c]
            if masked:
                q0 = (c * cw) % tq
                nk = q0 + cw
                kc = _iota((nk // CHUNK, 1, cw), 0)
                qc = (q0 + _iota((nk // CHUNK, 1, cw), 2)) // CHUNK
                bias = jnp.where(kc <= qc, 0.0, MASK_VALUE)
                s = (s_ref[c, 0:nk, :].reshape(nk // CHUNK, CHUNK, cw) + bias).reshape(nk, cw)
                vt = vtj[:, 0:nk]
            else:
                s = s_ref[c]
                vt = vtj
            m_new = jnp.maximum(m, jnp.max(s, axis=0, keepdims=True))
            p = jnp.exp2(s - m_new)
            alpha = jnp.exp2(m - m_new)
            l = alpha * l + jnp.sum(p, axis=0, keepdims=True)
            acc = acc * alpha + _dot(vt, p.astype(BF16))
            out.append((m_new, l, acc))
        return tuple(out)

    def score_next_block():
        scores(0, sc_ref, chains(qn_ref))

    def pair(j, stats):
        stats = absorb(j, sa_ref, stats, False, sb_ref)
        return absorb(j + 1, sb_ref, stats, False, sa_ref)

    def quad(jj, stats):
        return pair(4 * jj + 3, pair(4 * jj + 1, stats))

    def last_even(stats):
        score_next_block()
        return absorb(qi, sa_ref, stats, True)

    def last_odd(stats):
        stats = absorb(qi - 1, sa_ref, stats, False, sb_ref)
        score_next_block()
        return absorb(qi, sb_ref, stats, True)

    init = tuple((jnp.full((1, cw), MASK_VALUE, F32), jnp.zeros((1, cw), F32),
                  jnp.zeros((DIFF_V_DIM, cw), F32)) for _ in range(nch))

    def first_block(stats):
        scores(0, sc_ref, q_chains)
        stats = absorb(0, sc_ref, stats, True)
        score_next_block()
        return stats

    def later_block(stats):
        rest = qi - 1
        stats = absorb(0, sc_ref, stats, False, sa_ref)
        stats = lax.fori_loop(0, rest // 4, quad, stats)
        stats = lax.cond(rest % 4 >= 2, lambda st: pair(4 * (rest // 4) + 1, st), lambda st: st, stats)
        return lax.cond(rest % 2 == 1, last_odd, last_even, stats)

    carry = lax.cond(qi == 0, first_block, later_block, init)

    lam = (jnp.exp(jnp.sum(lq1_ref[...] * lk1_ref[...], keepdims=True))
           - jnp.exp(jnp.sum(lq2_ref[...] * lk2_ref[...], keepdims=True)) + lambda_init)
    on = jnp.concatenate([acc / l for (_, l, acc) in carry], axis=1)
    o = on[:, :tq] - lam * on[:, tq:]
    ms = jnp.mean(o * o, axis=0, keepdims=True)
    o = o * lax.rsqrt(ms + DIFF_SUBLN_EPS)
    o_ref[...] = (o.T * ng_ref[...] * (1.0 - lambda_init)).astype(o_ref.dtype)


def _flash(q_r, k_r, v_t, lq1, lk1, lq2, lk2, norm_g, bsz, s_len, tq, lambda_init):
    nq = s_len // tq
    cw = min(256, tq)
    vec = pl.BlockSpec((1, DIFF_HEAD_DIM), lambda b, h, i: (0, 0))
    return pl.pallas_call(
        functools.partial(_flash_kernel, tq=tq, cw=cw, lambda_init=lambda_init),
        grid=(bsz, DIFF_HEADS, nq),
        in_specs=[
            pl.BlockSpec((tq, LANES), lambda b, h, i: (b * nq + i, h)),
            pl.BlockSpec((tq, LANES), lambda b, h, i: (b * nq + jnp.minimum(i + 1, nq - 1), h)),
            pl.BlockSpec((s_len, LANES), lambda b, h, i: (b, h)),
            pl.BlockSpec((1, 1, nq, DIFF_V_DIM, tq), lambda b, h, i: (b, h, 0, 0, 0)),
            vec, vec, vec, vec,
            pl.BlockSpec((1, DIFF_V_DIM), lambda b, h, i: (0, 0)),
        ],
        out_specs=pl.BlockSpec((tq, DIFF_V_DIM), lambda b, h, i: (b * nq + i, h)),
        out_shape=jax.ShapeDtypeStruct((bsz * s_len, DIFF_V_WIDTH), BF16),
        scratch_shapes=[pltpu.VMEM((2 * tq // cw, tq, cw), F32)] * 3,
        compiler_params=_cparams(("parallel", "parallel", "arbitrary")),
        name="flash",
    )(q_r, q_r, k_r, v_t, lq1, lk1, lq2, lk2, norm_g)


def _merge_kernel(x_ref, yg_ref, ys_ref, yd_ref, g0_ref, g1_ref, g2_ref, bg_ref,
                  wg_ref, ws_ref, wd_ref, wo_ref, o_ref):
    def gate(g_ref, i):
        return 1.0 / (1.0 + jnp.exp(-(g_ref[...].astype(F32) + bg_ref[i:i + 1, :])))

    mixed = gate(g0_ref, 0) * _dot(yg_ref[...], wg_ref[...])
    mixed = mixed + gate(g1_ref, 1) * _dot(ys_ref[...], ws_ref[...])
    mixed = mixed + gate(g2_ref, 2) * _dot(yd_ref[...], wd_ref[...])
    o_ref[...] = x_ref[...] + _dot(mixed.astype(BF16), wo_ref[...])


def _merge(x2, y_gla, y_ssm, y_diff, slab, b_gate, w_g, w_s, w_d, w_o, layer, tm=512):
    m = x2.shape[0]
    gcol = COL_GATE // D_MODEL
    row = lambda w: pl.BlockSpec((tm, w), lambda i: (i, 0))
    const = lambda r, c: pl.BlockSpec((r, c), lambda i: (0, 0))
    weight = lambda r, c: pl.BlockSpec((None, r, c), lambda i: (layer, 0, 0))
    return pl.pallas_call(
        _merge_kernel,
        grid=(m // tm,),
        in_specs=[
            row(D_MODEL), row(GLA_VAL_DIM), row(SSM_D_INNER), row(DIFF_V_WIDTH),
            pl.BlockSpec((tm, D_MODEL), lambda i: (i, gcol)),
            pl.BlockSpec((tm, D_MODEL), lambda i: (i, gcol + 1)),
            pl.BlockSpec((tm, D_MODEL), lambda i: (i, gcol + 2)),
            const(N_BRANCHES, D_MODEL),
            weight(GLA_VAL_DIM, D_MODEL), weight(SSM_D_INNER, D_MODEL), weight(DIFF_V_WIDTH, D_MODEL),
            weight(D_MODEL, D_MODEL),
        ],
        out_specs=row(D_MODEL),
        out_shape=jax.ShapeDtypeStruct((m, D_MODEL), F32),
        compiler_params=_cparams(("parallel",)),
        name="merge",
    )(x2, y_gla, y_ssm, y_diff, slab, slab, slab, b_gate, w_g, w_s, w_d, w_o)


def _mlp_kernel(x_ref, g_ref, wu_ref, wd_ref, gf_ref, o_ref, *, final_norm, fc):
    x = x_ref[...]
    ms = jnp.mean(x * x, axis=-1, keepdims=True)
    h = (x * lax.rsqrt(ms + EPS) * g_ref[...]).astype(BF16)
    acc = x
    for c0 in range(0, D_FF, fc):
        u = jnp.maximum(_dot(h, wu_ref[:, c0:c0 + fc]), 0.0)
        acc = acc + _dot((u * u).astype(BF16), wd_ref[c0:c0 + fc, :])
    if final_norm:
        ms = jnp.mean(acc * acc, axis=-1, keepdims=True)
        acc = acc * lax.rsqrt(ms + EPS) * gf_ref[...]
    o_ref[...] = acc


def _mlp(x2, g, w_up, w_down, g_final, final_norm, layer, tm=512, fc=1024):
    m = x2.shape[0]
    row = pl.BlockSpec((tm, D_MODEL), lambda i: (i, 0))
    const = lambda r, c: pl.BlockSpec((r, c), lambda i: (0, 0))
    weight = lambda r, c: pl.BlockSpec((None, r, c), lambda i: (layer, 0, 0))
    return pl.pallas_call(
        functools.partial(_mlp_kernel, final_norm=final_norm, fc=fc),
        grid=(m // tm,),
        in_specs=[row, const(1, D_MODEL), weight(D_MODEL, D_FF), weight(D_FF, D_MODEL), const(1, D_MODEL)],
        out_specs=row,
        out_shape=jax.ShapeDtypeStruct((m, D_MODEL), F32),
        compiler_params=_cparams(("parallel",)),
        name="mlp",
    )(x2, g, w_up, w_down, g_final)


def _slab_weights(w_in):
    sizes = (GLA_KEY_DIM, GLA_KEY_DIM, GLA_VAL_DIM, GLA_LOW_RANK, GLA_VAL_DIM,
             SSM_D_INNER, SSM_CONV_DIM, SSM_HEADS,
             DIFF_QK_WIDTH, DIFF_QK_WIDTH, DIFF_V_WIDTH, N_BRANCHES * D_MODEL)
    offs = np.concatenate([[0], np.cumsum(sizes)])
    (a_q, a_k, a_v, a_gk, a_g, b_z, b_xbc, b_dt, c_q, c_k, c_v, gate) = [
        w_in[:, :, int(offs[i]):int(offs[i + 1])] for i in range(len(sizes))]
    slab = jnp.concatenate([b_xbc, gate, a_v, a_g, c_q, c_k, c_v, a_q, a_k, b_z], axis=2).astype(BF16)
    zeros = lambda n: jnp.zeros((w_in.shape[0], D_MODEL, n), w_in.dtype)
    small = jnp.concatenate([a_gk, zeros(LANES - GLA_LOW_RANK), b_dt, zeros(LANES - SSM_HEADS)],
                            axis=2).astype(BF16)
    return slab, small


def _pad_lanes(v, fill=0.0):
    v = v.reshape(1, -1).astype(F32)
    return jnp.pad(v, ((0, 0), (0, LANES - v.shape[1])), constant_values=fill)


def _forward(x, positions, norm_mix_g, w_in, b_gate, gla_w_gk2, gla_b_gk, gla_norm_g,
             ssm_conv_w, ssm_conv_b, ssm_dt_bias, ssm_a_log, ssm_d, ssm_norm_g,
             diff_lq1, diff_lk1, diff_lq2, diff_lk2, diff_norm_g,
             w_br_gla, w_br_ssm, w_br_diff, w_out, norm_mlp_g, w_mlp_up, w_mlp_down,
             norm_final_g, tq):
    bsz, s_len, _ = x.shape
    m = bsz * s_len
    depth = w_in.shape[0]
    x2 = x.reshape(m, D_MODEL).astype(F32)

    head_of_lane = np.arange(SSM_D_INNER) // SSM_HEAD_DIM
    expand = jnp.asarray((np.arange(LANES)[:, None] == head_of_lane[None, :]), BF16)
    pairsel = jnp.asarray((np.arange(LANES)[None, :] // 2) == np.arange(SSM_HEADS // 2)[:, None], BF16)
    pairsel = pairsel * jnp.asarray(np.arange(LANES)[None, :] < SSM_HEADS, BF16)
    t = np.arange((SSM_CONV_W - 1) * CHUNK)
    src_row = SSM_CONV_PAD + (t % CHUNK) - (SSM_CONV_W - 1) + (t // CHUNK)
    shift = jnp.asarray(np.arange(CHUNK + SSM_CONV_PAD)[None, :] == src_row[:, None], BF16)
    d = np.arange(LANES) % DIFF_HEAD_DIM
    inv_freq = ROPE_THETA ** (-(2.0 * (d % (ROT_DIM // 2))) / ROT_DIM)
    freq_row = jnp.asarray(np.where(d < ROT_DIM, inv_freq, 0.0).reshape(1, LANES), F32)

    cosf, s1, s2 = _rope_tables(positions.reshape(m, 1), freq_row, tr=min(512, s_len))

    w_slab, w_small = _slab_weights(w_in)
    wb_gla, wb_ssm, wb_diff, wb_out = (w.astype(BF16) for w in (w_br_gla, w_br_ssm, w_br_diff, w_out))
    wb_up, wb_down = w_mlp_up.astype(BF16), w_mlp_down.astype(BF16)

    for l in range(depth):
        slab, small = _inproj(x2, norm_mix_g[l].reshape(1, -1), w_slab, w_small, l,
                              tm=min(1024, m))

        wgk_pad = jnp.pad(gla_w_gk2[l].astype(F32), ((0, LANES - GLA_LOW_RANK), (0, 0)))
        b_all, dt, a_cum = _gates(small, wgk_pad, gla_b_gk[l].reshape(1, -1),
                                  _pad_lanes(ssm_dt_bias[l]), _pad_lanes(ssm_a_log[l]))
        y_gla = _gla(slab, b_all, gla_norm_g[l].reshape(1, -1), bsz, s_len)

        d_exp = jnp.repeat(ssm_d[l].astype(F32), SSM_HEAD_DIM).reshape(1, -1)
        y_ssm = _ssd(slab, dt, a_cum, ssm_conv_w[l], ssm_conv_b[l].reshape(1, -1), d_exp,
                     ssm_norm_g[l].reshape(1, -1), expand, pairsel, shift, bsz, s_len)

        lambda_init = 0.8 - 0.6 * math.exp(-0.3 * l)
        q_r, k_r, v_t = _attn_prep(slab, cosf, s1, s2, bsz, s_len, tq)
        y_diff = _flash(q_r, k_r, v_t, diff_lq1[l].reshape(1, -1), diff_lk1[l].reshape(1, -1),
                        diff_lq2[l].reshape(1, -1), diff_lk2[l].reshape(1, -1),
                        diff_norm_g[l].reshape(1, -1), bsz, s_len, tq, lambda_init)

        x2 = _merge(x2, y_gla, y_ssm, y_diff, slab, b_gate[l].reshape(N_BRANCHES, D_MODEL),
                    wb_gla, wb_ssm, wb_diff, wb_out, l, tm=min(512, m))
        x2 = _mlp(x2, norm_mlp_g[l].reshape(1, -1), wb_up, wb_down, norm_final_g.reshape(1, -1),
                  final_norm=(l == depth - 1), layer=l, tm=min(512, m))
    return x2.reshape(bsz, s_len, D_MODEL)


def kernel(x, positions, norm_mix_g, w_in, b_gate, gla_w_gk2, gla_b_gk, gla_norm_g, ssm_conv_w,
           ssm_conv_b, ssm_dt_bias, ssm_a_log, ssm_d, ssm_norm_g, diff_lq1, diff_lk1, diff_lq2,
           diff_lk2, diff_norm_g, w_br_gla, w_br_ssm, w_br_diff, w_out, norm_mlp_g, w_mlp_up,
           w_mlp_down, norm_final_g):
    return _forward(x, positions, norm_mix_g, w_in, b_gate, gla_w_gk2, gla_b_gk, gla_norm_g,
                    ssm_conv_w, ssm_conv_b, ssm_dt_bias, ssm_a_log, ssm_d, ssm_norm_g,
                    diff_lq1, diff_lk1, diff_lq2, diff_lk2, diff_norm_g,
                    w_br_gla, w_br_ssm, w_br_diff, w_out, norm_mlp_g, w_mlp_up, w_mlp_down,
                    norm_final_g, tq=min(512, x.shape[1]))
```

```python
import functools
import math

import jax
import jax.numpy as jnp
import numpy as np
from jax import lax
from jax.experimental import pallas as pl
from jax.experimental.pallas import tpu as pltpu

F32 = jnp.float32
BF16 = jnp.bfloat16

D_MODEL = 1024
DEPTH = 2
CHUNK = 64
ROPE_THETA = 500000.0
EPS = 1e-6

GLA_HEADS = 4
GLA_KEY_DIM = D_MODEL // 2
GLA_VAL_DIM = D_MODEL
GLA_DK = GLA_KEY_DIM // GLA_HEADS
GLA_DV = GLA_VAL_DIM // GLA_HEADS
GLA_LOW_RANK = 16
GLA_GATE_NORMALIZER = 16.0

SSM_D_INNER = 2 * D_MODEL
SSM_HEAD_DIM = 64
SSM_HEADS = SSM_D_INNER // SSM_HEAD_DIM
SSM_GROUPS = 4
SSM_HEADS_PER_GROUP = SSM_HEADS // SSM_GROUPS
SSM_STATE = 128
SSM_CONV_W = 4
SSM_BC_WIDTH = SSM_GROUPS * SSM_STATE
SSM_CONV_DIM = SSM_D_INNER + 2 * SSM_BC_WIDTH
SSM_GROUP_WIDTH = SSM_D_INNER // SSM_GROUPS
SSM_CONV_PAD = 16
RECUR_CHUNKS_PER_STEP = 4

DIFF_HEADS = 8
DIFF_HEAD_DIM = 64
DIFF_V_DIM = 2 * DIFF_HEAD_DIM
DIFF_QK_WIDTH = DIFF_HEADS * 2 * DIFF_HEAD_DIM
DIFF_V_WIDTH = DIFF_HEADS * DIFF_V_DIM
ROT_DIM = DIFF_HEAD_DIM // 4
DIFF_SUBLN_EPS = 1e-5

N_BRANCHES = 3
D_FF = 4 * D_MODEL

LANES = 128
MASK_VALUE = -1e30
QK_SCALE_LOG2 = DIFF_HEAD_DIM ** -0.5 * math.log2(math.e)

COL_XBC = 0
COL_GATE = COL_XBC + SSM_CONV_DIM
COL_GLA_V = COL_GATE + N_BRANCHES * D_MODEL
COL_GLA_G = COL_GLA_V + GLA_VAL_DIM
COL_DIFF_Q = COL_GLA_G + GLA_VAL_DIM
COL_DIFF_K = COL_DIFF_Q + DIFF_QK_WIDTH
COL_DIFF_V = COL_DIFF_K + DIFF_QK_WIDTH
COL_GLA_Q = COL_DIFF_V + DIFF_V_WIDTH
COL_GLA_K = COL_GLA_Q + GLA_KEY_DIM
COL_SSM_Z = COL_GLA_K + GLA_KEY_DIM
SLAB_COLS = COL_SSM_Z + SSM_D_INNER
SMALL_COLS = 2 * LANES

VMEM_LIMIT = 56 * 1024 * 1024


def _cparams(sem):
    return pltpu.CompilerParams(dimension_semantics=sem, vmem_limit_bytes=VMEM_LIMIT)


def _dot(a, b):
    return jnp.dot(a, b, preferred_element_type=F32)


def _dot_nt(a, b):
    return lax.dot_general(a, b, (((1,), (1,)), ((), ())), preferred_element_type=F32)


def _split3(x):
    hi = x.astype(BF16)
    r1 = x - hi.astype(F32)
    mid = r1.astype(BF16)
    lo = (r1 - mid.astype(F32)).astype(BF16)
    return hi, mid, lo


def _split2(x):
    hi = x.astype(BF16)
    return hi, (x - hi.astype(F32)).astype(BF16)


def _dot_sel_lhs2(sel, x):
    hi, mid = _split2(x)
    return _dot(sel, hi) + _dot(sel, mid)


def _dot_sel_rhs2(x, sel):
    hi, mid = _split2(x)
    return _dot(hi, sel) + _dot(mid, sel)


def _dot_nt_sel_lhs2(sel, x):
    hi, mid = _split2(x)
    return _dot_nt(sel, hi) + _dot_nt(sel, mid)


def _dot_f32(a, b):
    ah, am, _ = _split3(a)
    bh, bm, _ = _split3(b)
    return _dot(ah, bh) + _dot(ah, bm) + _dot(am, bh)


def _silu(x):
    return x / (1.0 + jnp.exp(-x))


def _log1p_exp_neg_abs(x):
    return jnp.log(1.0 + jnp.exp(-jnp.abs(x)))


def _log_sigmoid(x):
    return jnp.minimum(x, 0.0) - _log1p_exp_neg_abs(x)


def _softplus(x):
    return jnp.maximum(x, 0.0) + _log1p_exp_neg_abs(x)


def _iota(shape, dim):
    return lax.broadcasted_iota(jnp.int32, shape, dim)


def _inproj_kernel(x_ref, g_ref, w_ref, ws_ref, o_ref, os_ref, h_ref):
    @pl.when(pl.program_id(1) == 0)
    def _():
        x = x_ref[...]
        ms = jnp.mean(x * x, axis=-1, keepdims=True)
        h = (x * lax.rsqrt(ms + EPS) * g_ref[...]).astype(BF16)
        h_ref[...] = h
        os_ref[...] = _dot(h, ws_ref[...])

    o_ref[...] = _dot(h_ref[...], w_ref[...]).astype(o_ref.dtype)


def _inproj(x2, g, w_slab, w_small, layer, tm=1024, tn=2048):
    m = x2.shape[0]
    return pl.pallas_call(
        _inproj_kernel,
        grid=(m // tm, SLAB_COLS // tn),
        in_specs=[
            pl.BlockSpec((tm, D_MODEL), lambda i, j: (i, 0)),
            pl.BlockSpec((1, D_MODEL), lambda i, j: (0, 0)),
            pl.BlockSpec((None, D_MODEL, tn), lambda i, j: (layer, 0, j)),
            pl.BlockSpec((None, D_MODEL, SMALL_COLS), lambda i, j: (layer, 0, 0)),
        ],
        out_specs=[
            pl.BlockSpec((tm, tn), lambda i, j: (i, j)),
            pl.BlockSpec((tm, SMALL_COLS), lambda i, j: (i, 0)),
        ],
        out_shape=[
            jax.ShapeDtypeStruct((m, SLAB_COLS), BF16),
            jax.ShapeDtypeStruct((m, SMALL_COLS), F32),
        ],
        scratch_shapes=[pltpu.VMEM((tm, D_MODEL), BF16)],
        compiler_params=_cparams(("parallel", "arbitrary")),
        name="inproj",
    )(x2, g, w_slab, w_small)


def _gates_kernel(sm_ref, wgk_ref, bgk_ref, dtb_ref, alog_ref, b_ref, dt_ref, acum_ref):
    rows = sm_ref.shape[0]
    r = _iota((rows, rows), 0)
    c = _iota((rows, rows), 1)
    tril = jnp.where((c <= r) & ((r // CHUNK) == (c // CHUNK)), 1.0, 0.0).astype(BF16)

    gk = _log_sigmoid(_dot_f32(sm_ref[:, :LANES], wgk_ref[...]) + bgk_ref[...]) / GLA_GATE_NORMALIZER
    b_ref[...] = _dot_sel_lhs2(tril, gk)

    dt = _softplus(sm_ref[:, LANES:] + dtb_ref[...])
    dt_ref[...] = dt
    acum_ref[...] = _dot_sel_lhs2(tril, dt * (-jnp.exp(alog_ref[...])))


def _gates(small, wgk_pad, b_gk, dtb_pad, alog_pad, rows=256):
    m = small.shape[0]
    rows = min(rows, m)
    const = lambda r, c: pl.BlockSpec((r, c), lambda i: (0, 0))
    out = lambda w: pl.BlockSpec((rows, w), lambda i: (i, 0))
    return pl.pallas_call(
        _gates_kernel,
        grid=(m // rows,),
        in_specs=[pl.BlockSpec((rows, SMALL_COLS), lambda i: (i, 0)),
                  const(LANES, GLA_KEY_DIM), const(1, GLA_KEY_DIM), const(1, LANES), const(1, LANES)],
        out_specs=[out(GLA_KEY_DIM), out(LANES), out(LANES)],
        out_shape=[jax.ShapeDtypeStruct((m, GLA_KEY_DIM), F32),
                   jax.ShapeDtypeStruct((m, LANES), F32),
                   jax.ShapeDtypeStruct((m, LANES), F32)],
        compiler_params=_cparams(("parallel",)),
        name="gates",
    )(small, wgk_pad, b_gk, dtb_pad, alog_pad)


def _level_anchor(b, h):
    c = CHUNK
    if 2 * h >= 8:
        parts = [jnp.broadcast_to(b[s + h - 1:s + h, :], (2 * h, b.shape[1]))
                 for s in range(0, c, 2 * h)]
        return parts[0] if len(parts) == 1 else jnp.concatenate(parts, axis=0)
    pos = _iota(b.shape, 0) % (2 * h)
    if h == 1:
        return jnp.where(pos == 1, pltpu.roll(b, 1, 0), b)
    return jnp.where(pos == 0, pltpu.roll(b, c - 1, 0),
                     jnp.where(pos == 1, b,
                               jnp.where(pos == 2, pltpu.roll(b, 1, 0), pltpu.roll(b, 2, 0))))


def _gla_kernel(q_ref, k_ref, v_ref, g_ref, b_ref, ng_ref, o_ref, st_ref):
    @pl.when(pl.program_id(1) == 0)
    def _():
        st_ref[...] = jnp.zeros_like(st_ref)

    for ci in range(q_ref.shape[0] // CHUNK):
        rows = slice(ci * CHUNK, (ci + 1) * CHUNK)
        _gla_chunk(q_ref.at[rows], k_ref.at[rows], v_ref.at[rows], g_ref.at[rows], b_ref.at[rows],
                   ng_ref, o_ref.at[rows], st_ref)


def _gla_chunk(q_ref, k_ref, v_ref, g_ref, b_ref, ng_ref, o_ref, st_ref):
    c = CHUNK
    b_all = b_ref[...]
    row = _iota((c, c), 0)
    col = _iota((c, c), 1)
    eye = row == col

    heads = range(GLA_HEADS)
    ksl = [slice(hd * GLA_DK, (hd + 1) * GLA_DK) for hd in heads]
    vsl = [slice(hd * GLA_DV, (hd + 1) * GLA_DV) for hd in heads]
    q = q_ref[...].astype(F32) * (GLA_DK ** -0.5)
    k = k_ref[...].astype(F32)
    b = b_all
    b_last = b[c - 1:c, :]
    st = [st_ref[hd] for hd in heads]

    qg = (q * jnp.exp(b)).astype(BF16)
    kd = (k * jnp.exp(b_last - b)).astype(BF16)
    e_last = jnp.exp(b_last)
    o = [_dot_nt(qg[:, ksl[hd]], st[hd].astype(BF16)) for hd in heads]
    for hd in heads:
        vt = v_ref[:, vsl[hd]].astype(F32).T.astype(BF16)
        st_ref[hd] = st[hd] * e_last[:, ksl[hd]] + _dot(vt, kd[:, ksl[hd]])

    q16 = q.astype(BF16)
    k16 = k.astype(BF16)
    scores = [jnp.where(eye, _dot_nt(q16[:, ksl[hd]], k16[:, ksl[hd]]), 0.0) for hd in heads]
    h = c // 2
    while h >= 1:
        anchor = _level_anchor(b, h)
        second = (_iota(b.shape, 0) % (2 * h)) >= h
        decay = jnp.exp(jnp.where(second, b - anchor, anchor - b))
        qt = jnp.where(second, q * decay, 0.0).astype(BF16)
        kt = jnp.where(second, 0.0, k * decay).astype(BF16)
        same_block = (row // (2 * h)) == (col // (2 * h))
        for hd in heads:
            p = _dot_nt(qt[:, ksl[hd]], kt[:, ksl[hd]])
            if 2 * h < c:
                p = jnp.where(same_block, p, 0.0)
            scores[hd] = scores[hd] + p
        h //= 2

    for hd in heads:
        oh = o[hd] + _dot(scores[hd].astype(BF16), v_ref[:, vsl[hd]])
        g = g_ref[:, vsl[hd]].astype(F32)
        ms = jnp.mean(oh * oh, axis=-1, keepdims=True)
        o_ref[:, vsl[hd]] = (oh * lax.rsqrt(ms + EPS) * ng_ref[...] * _silu(g)).astype(o_ref.dtype)


def _gla(slab, b_all, norm_g, bsz, s_len):
    tr = RECUR_CHUNKS_PER_STEP * CHUNK
    nc = s_len // tr
    rb = lambda b, i: b * nc + i
    return pl.pallas_call(
        _gla_kernel,
        grid=(bsz, nc),
        in_specs=[
            pl.BlockSpec((tr, GLA_KEY_DIM), lambda b, i: (rb(b, i), COL_GLA_Q // GLA_KEY_DIM)),
            pl.BlockSpec((tr, GLA_KEY_DIM), lambda b, i: (rb(b, i), COL_GLA_K // GLA_KEY_DIM)),
            pl.BlockSpec((tr, GLA_VAL_DIM), lambda b, i: (rb(b, i), COL_GLA_V // GLA_VAL_DIM)),
            pl.BlockSpec((tr, GLA_VAL_DIM), lambda b, i: (rb(b, i), COL_GLA_G // GLA_VAL_DIM)),
            pl.BlockSpec((tr, GLA_KEY_DIM), lambda b, i: (rb(b, i), 0)),
            pl.BlockSpec((1, GLA_DV), lambda b, i: (0, 0)),
        ],
        out_specs=pl.BlockSpec((tr, GLA_VAL_DIM), lambda b, i: (rb(b, i), 0)),
        out_shape=jax.ShapeDtypeStruct((bsz * s_len, GLA_VAL_DIM), BF16),
        scratch_shapes=[pltpu.VMEM((GLA_HEADS, GLA_DV, GLA_DK), F32)],
        compiler_params=_cparams(("parallel", "arbitrary")),
        name="gla",
    )(slab, slab, slab, slab, b_all, norm_g)


def _ssd_kernel(xbc_ref, z_ref, dt_ref, acum_ref, cw_ref, cb_ref, dexp_ref, ng_ref,
                expand_ref, pairsel_ref, shift_ref, o_ref, xpad_ref, st_ref, yd_ref):
    pad = SSM_CONV_PAD
    tr = xbc_ref.shape[0]

    @pl.when(pl.program_id(1) == 0)
    def _():
        st_ref[...] = jnp.zeros_like(st_ref)
        xpad_ref[0:pad, :] = jnp.zeros((pad, SSM_CONV_DIM), BF16)

    xpad_ref[pad:pad + tr, :] = xbc_ref[...]
    for ci in range(tr // CHUNK):
        rows = slice(ci * CHUNK, (ci + 1) * CHUNK)
        _ssd_chunk(xpad_ref.at[ci * CHUNK:(ci + 1) * CHUNK + pad], xbc_ref.at[rows], z_ref.at[rows],
                   dt_ref.at[rows], acum_ref.at[rows], cw_ref, cb_ref, dexp_ref, ng_ref,
                   expand_ref, pairsel_ref, shift_ref, o_ref.at[rows], st_ref, yd_ref.at[rows])
    xpad_ref[0:pad, :] = xpad_ref[tr:tr + pad, :]


def _ssd_chunk(xwin_ref, xbc_ref, z_ref, dt_ref, acum_ref, cw_ref, cb_ref, dexp_ref, ng_ref,
               expand_ref, pairsel_ref, shift_ref, o_ref, st_ref, yd_ref):
    c = CHUNK
    n = SSM_STATE
    gw = SSM_GROUP_WIDTH

    delayed = _dot(shift_ref[...], xwin_ref[...])
    conv = cb_ref[...] + xbc_ref[...].astype(F32) * cw_ref[SSM_CONV_W - 1:SSM_CONV_W, :]
    for i in range(SSM_CONV_W - 1):
        conv = conv + delayed[i * c:(i + 1) * c] * cw_ref[i:i + 1, :]
    xc = _silu(conv)
    xs = xc[:, :SSM_D_INNER]

    a_cum = acum_ref[...]
    both = _dot_sel_rhs2(jnp.concatenate([dt_ref[...], a_cum], axis=0), expand_ref[...])
    dt_e = both[:c]
    a_e = both[c:]
    a_last_e = a_e[c - 1:c, :]
    xdt = xs * dt_e
    decay_out = jnp.exp(a_e)
    xdtd = (xdt * jnp.exp(a_last_e - a_e)).astype(BF16)
    chunk_decay = jnp.exp(a_last_e)

    lane = _iota((c, LANES), 1)
    z0 = jnp.concatenate([jnp.where(lane % 2 == 0, a_cum, 0.0),
                          jnp.where(lane % 2 == 1, a_cum, 0.0)], axis=0)
    pairs = _dot_nt_sel_lhs2(pairsel_ref[...], z0)

    lane2 = _iota((c, 2 * c), 1)
    row2 = _iota((c, 2 * c), 0)
    causal2 = (lane2 % c) <= row2
    bd_row = _iota((2 * c, 2 * c), 0)
    bd_col = _iota((2 * c, 2 * c), 1)
    bd_mask = (bd_row // c) == (bd_col // c)

    for g in range(SSM_GROUPS):
        bg = xc[:, SSM_D_INNER + g * n:SSM_D_INNER + (g + 1) * n]
        cg = xc[:, SSM_D_INNER + SSM_BC_WIDTH + g * n:SSM_D_INNER + SSM_BC_WIDTH + (g + 1) * n]
        cg16 = cg.astype(BF16)
        bg16 = bg.astype(BF16)
        gsl = slice(g * gw, (g + 1) * gw)
        st = st_ref[g]

        y_off = _dot(cg16, st.astype(BF16)) * decay_out[:, gsl]
        st_ref[g] = st * chunk_decay[:, gsl] + _dot(bg.T.astype(BF16), xdtd[:, gsl])

        cb2 = _dot_nt(cg16, jnp.concatenate([bg16, bg16], axis=0))
        for pr in range(SSM_HEADS_PER_GROUP // 2):
            pidx = g * (SSM_HEADS_PER_GROUP // 2) + pr
            lsl = slice(pidx * 2 * c, (pidx + 1) * 2 * c)
            seg = a_e[:, lsl] - pairs[pidx:pidx + 1, :]
            lmat = jnp.exp(jnp.where(causal2, seg, MASK_VALUE))
            w = (cb2 * lmat).astype(BF16)
            xp = xdt[:, lsl]
            bd = jnp.where(bd_mask, jnp.concatenate([xp, xp], axis=0), 0.0).astype(BF16)
            yd_ref[:, lsl] = _dot(w, bd)

        y = yd_ref[:, gsl] + y_off + xs[:, gsl] * dexp_ref[:, gsl]
        y = y * _silu(z_ref[:, gsl].astype(F32))
        ms = jnp.mean(y * y, axis=-1, keepdims=True)
        o_ref[:, gsl] = (y * lax.rsqrt(ms + EPS) * ng_ref[:, gsl]).astype(o_ref.dtype)


def _ssd(slab, dt, a_cum, conv_w, conv_b, d_exp, norm_g, expand, pairsel, shift, bsz, s_len):
    tr = RECUR_CHUNKS_PER_STEP * CHUNK
    nc = s_len // tr
    rb = lambda b, i: b * nc + i
    const = lambda b, i: (0, 0)
    return pl.pallas_call(
        _ssd_kernel,
        grid=(bsz, nc),
        in_specs=[
            pl.BlockSpec((tr, SSM_CONV_DIM), lambda b, i: (rb(b, i), COL_XBC // SSM_CONV_DIM)),
            pl.BlockSpec((tr, SSM_D_INNER), lambda b, i: (rb(b, i), COL_SSM_Z // SSM_D_INNER)),
            pl.BlockSpec((tr, LANES), lambda b, i: (rb(b, i), 0)),
            pl.BlockSpec((tr, LANES), lambda b, i: (rb(b, i), 0)),
            pl.BlockSpec((SSM_CONV_W, SSM_CONV_DIM), const),
            pl.BlockSpec((1, SSM_CONV_DIM), const),
            pl.BlockSpec((1, SSM_D_INNER), const),
            pl.BlockSpec((1, SSM_D_INNER), const),
            pl.BlockSpec((LANES, SSM_D_INNER), const),
            pl.BlockSpec((SSM_HEADS // 2, LANES), const),
            pl.BlockSpec(((SSM_CONV_W - 1) * CHUNK, CHUNK + SSM_CONV_PAD), const),
        ],
        out_specs=pl.BlockSpec((tr, SSM_D_INNER), lambda b, i: (rb(b, i), 0)),
        out_shape=jax.ShapeDtypeStruct((bsz * s_len, SSM_D_INNER), BF16),
        scratch_shapes=[
            pltpu.VMEM((tr + SSM_CONV_PAD, SSM_CONV_DIM), BF16),
            pltpu.VMEM((SSM_GROUPS, SSM_STATE, SSM_GROUP_WIDTH), F32),
            pltpu.VMEM((tr, SSM_D_INNER), F32),
        ],
        compiler_params=_cparams(("parallel", "arbitrary")),
        name="ssd",
    )(slab, slab, dt, a_cum, conv_w, conv_b, d_exp, norm_g, expand, pairsel, shift)


def _rope_table_kernel(pos_ref, freq_ref, c_ref, s1_ref, s2_ref):
    ang = pos_ref[...].astype(F32) * freq_ref[...]
    d = _iota(ang.shape, 1) % DIFF_HEAD_DIM
    half = ROT_DIM // 2
    cosv = jnp.cos(ang)
    sinv = jnp.sin(ang)
    c_ref[...] = jnp.where(d < ROT_DIM, cosv, 1.0)
    s1_ref[...] = jnp.where(d < half, -sinv, 0.0)
    s2_ref[...] = jnp.where((d >= half) & (d < ROT_DIM), sinv, 0.0)


def _rope_tables(pos_col, freq_row, tr=512):
    m = pos_col.shape[0]
    spec = pl.BlockSpec((tr, LANES), lambda i: (i, 0))
    shp = jax.ShapeDtypeStruct((m, LANES), F32)
    return pl.pallas_call(
        _rope_table_kernel,
        grid=(m // tr,),
        in_specs=[pl.BlockSpec((tr, 1), lambda i: (i, 0)), pl.BlockSpec((1, LANES), lambda i: (0, 0))],
        out_specs=[spec, spec, spec],
        out_shape=[shp, shp, shp],
        compiler_params=_cparams(("parallel",)),
        name="rope_tables",
    )(pos_col, freq_row)


def _attn_prep_kernel(q_ref, k_ref, v_ref, c_ref, s1_ref, s2_ref, qt_ref, ko_ref, vt_ref):
    half = ROT_DIM // 2
    cosf = c_ref[...]
    s1 = s1_ref[...]
    s2 = s2_ref[...]

    def rope(t):
        return t * cosf + pltpu.roll(t, LANES - half, 1) * s1 + pltpu.roll(t, half, 1) * s2

    for hd in range(DIFF_HEADS):
        sl = slice(hd * LANES, (hd + 1) * LANES)
        q = q_ref[:, sl].astype(F32)
        k = k_ref[:, sl].astype(F32)
        qt_ref[0, hd, 0] = (rope(q) * QK_SCALE_LOG2).T.astype(BF16)
        ko_ref[:, sl] = rope(k).astype(BF16)
        vt_ref[0, hd, 0] = v_ref[:, sl].astype(F32).T.astype(BF16)


def _attn_prep(slab, cosf, s1, s2, bsz, s_len, tr):
    m = bsz * s_len
    nb = s_len // tr
    w = DIFF_QK_WIDTH
    row = pl.BlockSpec((tr, w), lambda i: (i, 0))
    tab = pl.BlockSpec((tr, LANES), lambda i: (i, 0))
    tposed = pl.BlockSpec((1, DIFF_HEADS, 1, LANES, tr), lambda i: (i // nb, 0, i % nb, 0, 0))
    return pl.pallas_call(
        _attn_prep_kernel,
        grid=(m // tr,),
        in_specs=[
            pl.BlockSpec((tr, w), lambda i: (i, COL_DIFF_Q // w)),
            pl.BlockSpec((tr, w), lambda i: (i, COL_DIFF_K // w)),
            pl.BlockSpec((tr, w), lambda i: (i, COL_DIFF_V // w)),
            tab, tab, tab,
        ],
        out_specs=[tposed, row, tposed],
        out_shape=[
            jax.ShapeDtypeStruct((bsz, DIFF_HEADS, nb, LANES, tr), BF16),
            jax.ShapeDtypeStruct((m, w), BF16),
            jax.ShapeDtypeStruct((bsz, DIFF_HEADS, nb, DIFF_V_DIM, tr), BF16),
        ],
        compiler_params=_cparams(("parallel",)),
        name="attn_prep",
    )(slab, slab, slab, cosf, s1, s2)


def _flash_kernel(q_ref, qn_ref, k_ref, vt_ref, lq1_ref, lk1_ref, lq2_ref, lk2_ref, ng_ref, o_ref,
                  sa_ref, sb_ref, sc_ref, *, tq, cw, lambda_init):
    qi = pl.program_id(2)
    nch = 2 * tq // cw
    lead = 1

    def chains(ref):
        qt = ref[0, 0, 0]
        feat = _iota(qt.shape, 0)
        zero = jnp.zeros_like(qt)
        qs = jnp.concatenate([jnp.where(feat < DIFF_HEAD_DIM, qt, zero),
                              jnp.where(feat >= DIFF_HEAD_DIM, qt, zero)], axis=1)
        return [qs[:, c * cw:(c + 1) * cw] for c in range(nch)]

    q_chains = chains(q_ref)

    def keys(j):
        return k_ref[pl.ds(pl.multiple_of(j * tq, tq), tq), :]

    def scores(j, s_ref, qc):
        kj = keys(j)
        for c in range(nch):
            s_ref[c] = _dot(kj, qc[c])

    def absorb(j, s_ref, stats, masked, nxt_ref=None):
        vtj = vt_ref[0, 0, j]
        kn = None if nxt_ref is None else keys(j + 1)
        out = []
        if nxt_ref is not None:
            for c in range(min(lead, nch)):
                nxt_ref[c] = _dot(kn, q_chains[c])
        for c in range(nch):
            if nxt_ref is not None and c + lead < nch:
                nxt_ref[c + lead] = _dot(kn, q_chains[c + lead])
            m, l, acc = stats[c]
            if masked:
                q0 = (c * cw) % tq
                nk = q0 + cw
                kc = _iota((nk // CHUNK, 1, cw), 0)
                qc = (q0 + _iota((nk // CHUNK, 1, cw), 2)) // CHUNK
                bias = jnp.where(kc <= qc, 0.0, MASK_VALUE)
                s = (s_ref[c, 0:nk, :].reshape(nk // CHUNK, CHUNK, cw) + bias).reshape(nk, cw)
                vt = vtj[:, 0:nk]
            else:
                s = s_ref[c]
                vt = vtj
            m_new = jnp.maximum(m, jnp.max(s, axis=0, keepdims=True))
            p = jnp.exp2(s - m_new)
            alpha = jnp.exp2(m - m_new)
            l = alpha * l + jnp.sum(p, axis=0, keepdims=True)
            acc = acc * alpha + _dot(vt, p.astype(BF16))
            out.append((m_new, l, acc))
        return tuple(out)

    def score_next_block():
        scores(0, sc_ref, chains(qn_ref))

    def pair(j, stats):
        stats = absorb(j, sa_ref, stats, False, sb_ref)
        return absorb(j + 1, sb_ref, stats, False, sa_ref)

    def quad(jj, stats):
        return pair(4 * jj + 3, pair(4 * jj + 1, stats))

    def last_even(stats):
        score_next_block()
        return absorb(qi, sa_ref, stats, True)

    def last_odd(stats):
        stats = absorb(qi - 1, sa_ref, stats, False, sb_ref)
        score_next_block()
        return absorb(qi, sb_ref, stats, True)

    init = tuple((jnp.full((1, cw), MASK_VALUE, F32), jnp.zeros((1, cw), F32),
                  jnp.zeros((DIFF_V_DIM, cw), F32)) for _ in range(nch))

    def first_block(stats):
        scores(0, sc_ref, q_chains)
        stats = absorb(0, sc_ref, stats, True)
        score_next_block()
        return stats

    def later_block(stats):
        rest = qi - 1
        stats = absorb(0, sc_ref, stats, False, sa_ref)
        stats = lax.fori_loop(0, rest // 4, quad, stats)
        stats = lax.cond(rest % 4 >= 2, lambda st: pair(4 * (rest // 4) + 1, st), lambda st: st, stats)
        return lax.cond(rest % 2 == 1, last_odd, last_even, stats)

    carry = lax.cond(qi == 0, first_block, later_block, init)

    lam = (jnp.exp(jnp.sum(lq1_ref[...] * lk1_ref[...], keepdims=True))
           - jnp.exp(jnp.sum(lq2_ref[...] * lk2_ref[...], keepdims=True)) + lambda_init)
    on = jnp.concatenate([acc / l for (_, l, acc) in carry], axis=1)
    o = on[:, :tq] - lam * on[:, tq:]
    ms = jnp.mean(o * o, axis=0, keepdims=True)
    o = o * lax.rsqrt(ms + DIFF_SUBLN_EPS)
    o_ref[...] = (o.T * ng_ref[...] * (1.0 - lambda_init)).astype(o_ref.dtype)


def _flash(q_t, k_r, v_t, lq1, lk1, lq2, lk2, norm_g, bsz, s_len, tq, lambda_init):
    nq = s_len // tq
    cw = min(256, tq)
    vec = pl.BlockSpec((1, DIFF_HEAD_DIM), lambda b, h, i: (0, 0))
    return pl.pallas_call(
        functools.partial(_flash_kernel, tq=tq, cw=cw, lambda_init=lambda_init),
        grid=(bsz, DIFF_HEADS, nq),
        in_specs=[
            pl.BlockSpec((1, 1, 1, LANES, tq), lambda b, h, i: (b, h, i, 0, 0)),
            pl.BlockSpec((1, 1, 1, LANES, tq), lambda b, h, i: (b, h, jnp.minimum(i + 1, nq - 1), 0, 0)),
            pl.BlockSpec((s_len, LANES), lambda b, h, i: (b, h)),
            pl.BlockSpec((1, 1, nq, DIFF_V_DIM, tq), lambda b, h, i: (b, h, 0, 0, 0)),
            vec, vec, vec, vec,
            pl.BlockSpec((1, DIFF_V_DIM), lambda b, h, i: (0, 0)),
        ],
        out_specs=pl.BlockSpec((tq, DIFF_V_DIM), lambda b, h, i: (b * nq + i, h)),
        out_shape=jax.ShapeDtypeStruct((bsz * s_len, DIFF_V_WIDTH), BF16),
        scratch_shapes=[pltpu.VMEM((2 * tq // cw, tq, cw), F32)] * 3,
        compiler_params=_cparams(("parallel", "parallel", "arbitrary")),
        name="flash",
    )(q_t, q_t, k_r, v_t, lq1, lk1, lq2, lk2, norm_g)


def _merge_kernel(x_ref, yg_ref, ys_ref, yd_ref, g0_ref, g1_ref, g2_ref, bg_ref,
                  wg_ref, ws_ref, wd_ref, wo_ref, o_ref):
    def gate(g_ref, i):
        return 1.0 / (1.0 + jnp.exp(-(g_ref[...].astype(F32) + bg_ref[i:i + 1, :])))

    mixed = gate(g0_ref, 0) * _dot(yg_ref[...], wg_ref[...])
    mixed = mixed + gate(g1_ref, 1) * _dot(ys_ref[...], ws_ref[...])
    mixed = mixed + gate(g2_ref, 2) * _dot(yd_ref[...], wd_ref[...])
    o_ref[...] = x_ref[...] + _dot(mixed.astype(BF16), wo_ref[...])


def _merge(x2, y_gla, y_ssm, y_diff, slab, b_gate, w_g, w_s, w_d, w_o, layer, tm=512):
    m = x2.shape[0]
    gcol = COL_GATE // D_MODEL
    row = lambda w: pl.BlockSpec((tm, w), lambda i: (i, 0))
    const = lambda r, c: pl.BlockSpec((r, c), lambda i: (0, 0))
    weight = lambda r, c: pl.BlockSpec((None, r, c), lambda i: (layer, 0, 0))
    return pl.pallas_call(
        _merge_kernel,
        grid=(m // tm,),
        in_specs=[
            row(D_MODEL), row(GLA_VAL_DIM), row(SSM_D_INNER), row(DIFF_V_WIDTH),
            pl.BlockSpec((tm, D_MODEL), lambda i: (i, gcol)),
            pl.BlockSpec((tm, D_MODEL), lambda i: (i, gcol + 1)),
            pl.BlockSpec((tm, D_MODEL), lambda i: (i, gcol + 2)),
            const(N_BRANCHES, D_MODEL),
            weight(GLA_VAL_DIM, D_MODEL), weight(SSM_D_INNER, D_MODEL), weight(DIFF_V_WIDTH, D_MODEL),
            weight(D_MODEL, D_MODEL),
        ],
        out_specs=row(D_MODEL),
        out_shape=jax.ShapeDtypeStruct((m, D_MODEL), F32),
        compiler_params=_cparams(("parallel",)),
        name="merge",
    )(x2, y_gla, y_ssm, y_diff, slab, slab, slab, b_gate, w_g, w_s, w_d, w_o)


def _mlp_kernel(x_ref, g_ref, wu_ref, wd_ref, gf_ref, o_ref, *, final_norm, fc):
    x = x_ref[...]
    ms = jnp.mean(x * x, axis=-1, keepdims=True)
    h = (x * lax.rsqrt(ms + EPS) * g_ref[...]).astype(BF16)
    acc = x
    for c0 in range(0, D_FF, fc):
        u = jnp.maximum(_dot(h, wu_ref[:, c0:c0 + fc]), 0.0)
        acc = acc + _dot((u * u).astype(BF16), wd_ref[c0:c0 + fc, :])
    if final_norm:
        ms = jnp.mean(acc * acc, axis=-1, keepdims=True)
        acc = acc * lax.rsqrt(ms + EPS) * gf_ref[...]
    o_ref[...] = acc


def _mlp(x2, g, w_up, w_down, g_final, final_norm, layer, tm=512, fc=1024):
    m = x2.shape[0]
    row = pl.BlockSpec((tm, D_MODEL), lambda i: (i, 0))
    const = lambda r, c: pl.BlockSpec((r, c), lambda i: (0, 0))
    weight = lambda r, c: pl.BlockSpec((None, r, c), lambda i: (layer, 0, 0))
    return pl.pallas_call(
        functools.partial(_mlp_kernel, final_norm=final_norm, fc=fc),
        grid=(m // tm,),
        in_specs=[row, const(1, D_MODEL), weight(D_MODEL, D_FF), weight(D_FF, D_MODEL), const(1, D_MODEL)],
        out_specs=row,
        out_shape=jax.ShapeDtypeStruct((m, D_MODEL), F32),
        compiler_params=_cparams(("parallel",)),
        name="mlp",
    )(x2, g, w_up, w_down, g_final)


def _slab_weights(w_in):
    sizes = (GLA_KEY_DIM, GLA_KEY_DIM, GLA_VAL_DIM, GLA_LOW_RANK, GLA_VAL_DIM,
             SSM_D_INNER, SSM_CONV_DIM, SSM_HEADS,
             DIFF_QK_WIDTH, DIFF_QK_WIDTH, DIFF_V_WIDTH, N_BRANCHES * D_MODEL)
    offs = np.concatenate([[0], np.cumsum(sizes)])
    (a_q, a_k, a_v, a_gk, a_g, b_z, b_xbc, b_dt, c_q, c_k, c_v, gate) = [
        w_in[:, :, int(offs[i]):int(offs[i + 1])] for i in range(len(sizes))]
    slab = jnp.concatenate([b_xbc, gate, a_v, a_g, c_q, c_k, c_v, a_q, a_k, b_z], axis=2).astype(BF16)
    zeros = lambda n: jnp.zeros((w_in.shape[0], D_MODEL, n), w_in.dtype)
    small = jnp.concatenate([a_gk, zeros(LANES - GLA_LOW_RANK), b_dt, zeros(LANES - SSM_HEADS)],
                            axis=2).astype(BF16)
    return slab, small


def _pad_lanes(v, fill=0.0):
    v = v.reshape(1, -1).astype(F32)
    return jnp.pad(v, ((0, 0), (0, LANES - v.shape[1])), constant_values=fill)


def _forward(x, positions, norm_mix_g, w_in, b_gate, gla_w_gk2, gla_b_gk, gla_norm_g,
             ssm_conv_w, ssm_conv_b, ssm_dt_bias, ssm_a_log, ssm_d, ssm_norm_g,
             diff_lq1, diff_lk1, diff_lq2, diff_lk2, diff_norm_g,
             w_br_gla, w_br_ssm, w_br_diff, w_out, norm_mlp_g, w_mlp_up, w_mlp_down,
             norm_final_g, tq):
    bsz, s_len, _ = x.shape
    m = bsz * s_len
    depth = w_in.shape[0]
    x2 = x.reshape(m, D_MODEL).astype(F32)

    head_of_lane = np.arange(SSM_D_INNER) // SSM_HEAD_DIM
    expand = jnp.asarray((np.arange(LANES)[:, None] == head_of_lane[None, :]), BF16)
    pairsel = jnp.asarray((np.arange(LANES)[None, :] // 2) == np.arange(SSM_HEADS // 2)[:, None], BF16)
    pairsel = pairsel * jnp.asarray(np.arange(LANES)[None, :] < SSM_HEADS, BF16)
    t = np.arange((SSM_CONV_W - 1) * CHUNK)
    src_row = SSM_CONV_PAD + (t % CHUNK) - (SSM_CONV_W - 1) + (t // CHUNK)
    shift = jnp.asarray(np.arange(CHUNK + SSM_CONV_PAD)[None, :] == src_row[:, None], BF16)
    d = np.arange(LANES) % DIFF_HEAD_DIM
    inv_freq = ROPE_THETA ** (-(2.0 * (d % (ROT_DIM // 2))) / ROT_DIM)
    freq_row = jnp.asarray(np.where(d < ROT_DIM, inv_freq, 0.0).reshape(1, LANES), F32)

    cosf, s1, s2 = _rope_tables(positions.reshape(m, 1), freq_row, tr=min(512, s_len))

    w_slab, w_small = _slab_weights(w_in)
    wb_gla, wb_ssm, wb_diff, wb_out = (w.astype(BF16) for w in (w_br_gla, w_br_ssm, w_br_diff, w_out))
    wb_up, wb_down = w_mlp_up.astype(BF16), w_mlp_down.astype(BF16)

    for l in range(depth):
        slab, small = _inproj(x2, norm_mix_g[l].reshape(1, -1), w_slab, w_small, l,
                              tm=min(1024, m))

        wgk_pad = jnp.pad(gla_w_gk2[l].astype(F32), ((0, LANES - GLA_LOW_RANK), (0, 0)))
        b_all, dt, a_cum = _gates(small, wgk_pad, gla_b_gk[l].reshape(1, -1),
                                  _pad_lanes(ssm_dt_bias[l]), _pad_lanes(ssm_a_log[l]))
        y_gla = _gla(slab, b_all, gla_norm_g[l].reshape(1, -1), bsz, s_len)

        d_exp = jnp.repeat(ssm_d[l].astype(F32), SSM_HEAD_DIM).reshape(1, -1)
        y_ssm = _ssd(slab, dt, a_cum, ssm_conv_w[l], ssm_conv_b[l].reshape(1, -1), d_exp,
                     ssm_norm_g[l].reshape(1, -1), expand, pairsel, shift, bsz, s_len)

        lambda_init = 0.8 - 0.6 * math.exp(-0.3 * l)
        q_t, k_r, v_t = _attn_prep(slab, cosf, s1, s2, bsz, s_len, tq)
        y_diff = _flash(q_t, k_r, v_t, diff_lq1[l].reshape(1, -1), diff_lk1[l].reshape(1, -1),
                        diff_lq2[l].reshape(1, -1), diff_lk2[l].reshape(1, -1),
                        diff_norm_g[l].reshape(1, -1), bsz, s_len, tq, lambda_init)

        x2 = _merge(x2, y_gla, y_ssm, y_diff, slab, b_gate[l].reshape(N_BRANCHES, D_MODEL),
                    wb_gla, wb_ssm, wb_diff, wb_out, l, tm=min(512, m))
        x2 = _mlp(x2, norm_mlp_g[l].reshape(1, -1), wb_up, wb_down, norm_final_g.reshape(1, -1),
                  final_norm=(l == depth - 1), layer=l, tm=min(512, m))
    return x2.reshape(bsz, s_len, D_MODEL)


def kernel(x, positions, norm_mix_g, w_in, b_gate, gla_w_gk2, gla_b_gk, gla_norm_g, ssm_conv_w,
           ssm_conv_b, ssm_dt_bias, ssm_a_log, ssm_d, ssm_norm_g, diff_lq1, diff_lk1, diff_lq2,
           diff_lk2, diff_norm_g, w_br_gla, w_br_ssm, w_br_diff, w_out, norm_mlp_g, w_mlp_up,
           w_mlp_down, norm_final_g):
    return _forward(x, positions, norm_mix_g, w_in, b_gate, gla_w_gk2, gla_b_gk, gla_norm_g,
                    ssm_conv_w, ssm_conv_b, ssm_dt_bias, ssm_a_log, ssm_d, ssm_norm_g,
                    diff_lq1, diff_lk1, diff_lq2, diff_lk2, diff_norm_g,
                    w_br_gla, w_br_ssm, w_br_diff, w_out, norm_mlp_g, w_mlp_up, w_mlp_down,
                    norm_final_g, tq=min(512, x.shape[1]))
```

```python
import functools
import math

import jax
import jax.numpy as jnp
import numpy as np
from jax import lax
from jax.experimental import pallas as pl
from jax.experimental.pallas import tpu as pltpu

F32 = jnp.float32
BF16 = jnp.bfloat16

D_MODEL = 1024
DEPTH = 2
CHUNK = 64
ROPE_THETA = 500000.0
EPS = 1e-6

GLA_HEADS = 4
GLA_KEY_DIM = D_MODEL // 2
GLA_VAL_DIM = D_MODEL
GLA_DK = GLA_KEY_DIM // GLA_HEADS
GLA_DV = GLA_VAL_DIM // GLA_HEADS
GLA_LOW_RANK = 16
GLA_GATE_NORMALIZER = 16.0

SSM_D_INNER = 2 * D_MODEL
SSM_HEAD_DIM = 64
SSM_HEADS = SSM_D_INNER // SSM_HEAD_DIM
SSM_GROUPS = 4
SSM_HEADS_PER_GROUP = SSM_HEADS // SSM_GROUPS
SSM_STATE = 128
SSM_CONV_W = 4
SSM_BC_WIDTH = SSM_GROUPS * SSM_STATE
SSM_CONV_DIM = SSM_D_INNER + 2 * SSM_BC_WIDTH
SSM_GROUP_WIDTH = SSM_D_INNER // SSM_GROUPS
SSM_CONV_PAD = 16
RECUR_CHUNKS_PER_STEP = 4

DIFF_HEADS = 8
DIFF_HEAD_DIM = 64
DIFF_V_DIM = 2 * DIFF_HEAD_DIM
DIFF_QK_WIDTH = DIFF_HEADS * 2 * DIFF_HEAD_DIM
DIFF_V_WIDTH = DIFF_HEADS * DIFF_V_DIM
ROT_DIM = DIFF_HEAD_DIM // 4
DIFF_SUBLN_EPS = 1e-5
DIFF_V_ROWS = DIFF_V_DIM + 16

N_BRANCHES = 3
D_FF = 4 * D_MODEL

LANES = 128
MASK_VALUE = -1e30
QK_SCALE_LOG2 = DIFF_HEAD_DIM ** -0.5 * math.log2(math.e)

COL_XBC = 0
COL_GATE = COL_XBC + SSM_CONV_DIM
COL_GLA_V = COL_GATE + N_BRANCHES * D_MODEL
COL_GLA_G = COL_GLA_V + GLA_VAL_DIM
COL_DIFF_Q = COL_GLA_G + GLA_VAL_DIM
COL_DIFF_K = COL_DIFF_Q + DIFF_QK_WIDTH
COL_DIFF_V = COL_DIFF_K + DIFF_QK_WIDTH
COL_GLA_Q = COL_DIFF_V + DIFF_V_WIDTH
COL_GLA_K = COL_GLA_Q + GLA_KEY_DIM
COL_SSM_Z = COL_GLA_K + GLA_KEY_DIM
SLAB_COLS = COL_SSM_Z + SSM_D_INNER
SMALL_COLS = 2 * LANES

VMEM_LIMIT = 56 * 1024 * 1024


def _cparams(sem):
    return pltpu.CompilerParams(dimension_semantics=sem, vmem_limit_bytes=VMEM_LIMIT)


def _dot(a, b):
    return jnp.dot(a, b, preferred_element_type=F32)


def _dot_nt(a, b):
    return lax.dot_general(a, b, (((1,), (1,)), ((), ())), preferred_element_type=F32)


def _split3(x):
    hi = x.astype(BF16)
    r1 = x - hi.astype(F32)
    mid = r1.astype(BF16)
    lo = (r1 - mid.astype(F32)).astype(BF16)
    return hi, mid, lo


def _split2(x):
    hi = x.astype(BF16)
    return hi, (x - hi.astype(F32)).astype(BF16)


def _dot_sel_lhs2(sel, x):
    hi, mid = _split2(x)
    return _dot(sel, hi) + _dot(sel, mid)


def _dot_sel_rhs2(x, sel):
    hi, mid = _split2(x)
    return _dot(hi, sel) + _dot(mid, sel)


def _dot_nt_sel_lhs2(sel, x):
    hi, mid = _split2(x)
    return _dot_nt(sel, hi) + _dot_nt(sel, mid)


def _dot_f32(a, b):
    ah, am, _ = _split3(a)
    bh, bm, _ = _split3(b)
    return _dot(ah, bh) + _dot(ah, bm) + _dot(am, bh)


def _silu(x):
    return x / (1.0 + jnp.exp(-x))


def _log1p_exp_neg_abs(x):
    return jnp.log(1.0 + jnp.exp(-jnp.abs(x)))


def _log_sigmoid(x):
    return jnp.minimum(x, 0.0) - _log1p_exp_neg_abs(x)


def _softplus(x):
    return jnp.maximum(x, 0.0) + _log1p_exp_neg_abs(x)


def _iota(shape, dim):
    return lax.broadcasted_iota(jnp.int32, shape, dim)


def _inproj_kernel(x_ref, g_ref, w_ref, ws_ref, o_ref, os_ref, h_ref):
    @pl.when(pl.program_id(1) == 0)
    def _():
        x = x_ref[...]
        ms = jnp.mean(x * x, axis=-1, keepdims=True)
        h = (x * lax.rsqrt(ms + EPS) * g_ref[...]).astype(BF16)
        h_ref[...] = h
        os_ref[...] = _dot(h, ws_ref[...])

    o_ref[...] = _dot(h_ref[...], w_ref[...]).astype(o_ref.dtype)


def _inproj(x2, g, w_slab, w_small, layer, tm=1024, tn=2048):
    m = x2.shape[0]
    return pl.pallas_call(
        _inproj_kernel,
        grid=(m // tm, SLAB_COLS // tn),
        in_specs=[
            pl.BlockSpec((tm, D_MODEL), lambda i, j: (i, 0)),
            pl.BlockSpec((1, D_MODEL), lambda i, j: (0, 0)),
            pl.BlockSpec((None, D_MODEL, tn), lambda i, j: (layer, 0, j)),
            pl.BlockSpec((None, D_MODEL, SMALL_COLS), lambda i, j: (layer, 0, 0)),
        ],
        out_specs=[
            pl.BlockSpec((tm, tn), lambda i, j: (i, j)),
            pl.BlockSpec((tm, SMALL_COLS), lambda i, j: (i, 0)),
        ],
        out_shape=[
            jax.ShapeDtypeStruct((m, SLAB_COLS), BF16),
            jax.ShapeDtypeStruct((m, SMALL_COLS), F32),
        ],
        scratch_shapes=[pltpu.VMEM((tm, D_MODEL), BF16)],
        compiler_params=_cparams(("parallel", "arbitrary")),
        name="inproj",
    )(x2, g, w_slab, w_small)


def _gates_kernel(sm_ref, wgk_ref, bgk_ref, dtb_ref, alog_ref, b_ref, dt_ref, acum_ref):
    rows = sm_ref.shape[0]
    r = _iota((rows, rows), 0)
    c = _iota((rows, rows), 1)
    tril = jnp.where((c <= r) & ((r // CHUNK) == (c // CHUNK)), 1.0, 0.0).astype(BF16)

    gk = _log_sigmoid(_dot_f32(sm_ref[:, :LANES], wgk_ref[...]) + bgk_ref[...]) / GLA_GATE_NORMALIZER
    b_ref[...] = _dot_sel_lhs2(tril, gk)

    dt = _softplus(sm_ref[:, LANES:] + dtb_ref[...])
    dt_ref[...] = dt
    acum_ref[...] = _dot_sel_lhs2(tril, dt * (-jnp.exp(alog_ref[...])))


def _gates(small, wgk_pad, b_gk, dtb_pad, alog_pad, rows=256):
    m = small.shape[0]
    rows = min(rows, m)
    const = lambda r, c: pl.BlockSpec((r, c), lambda i: (0, 0))
    out = lambda w: pl.BlockSpec((rows, w), lambda i: (i, 0))
    return pl.pallas_call(
        _gates_kernel,
        grid=(m // rows,),
        in_specs=[pl.BlockSpec((rows, SMALL_COLS), lambda i: (i, 0)),
                  const(LANES, GLA_KEY_DIM), const(1, GLA_KEY_DIM), const(1, LANES), const(1, LANES)],
        out_specs=[out(GLA_KEY_DIM), out(LANES), out(LANES)],
        out_shape=[jax.ShapeDtypeStruct((m, GLA_KEY_DIM), F32),
                   jax.ShapeDtypeStruct((m, LANES), F32),
                   jax.ShapeDtypeStruct((m, LANES), F32)],
        compiler_params=_cparams(("parallel",)),
        name="gates",
    )(small, wgk_pad, b_gk, dtb_pad, alog_pad)


def _level_anchor(b, h):
    c = CHUNK
    if 2 * h >= 8:
        parts = [jnp.broadcast_to(b[s + h - 1:s + h, :], (2 * h, b.shape[1]))
                 for s in range(0, c, 2 * h)]
        return parts[0] if len(parts) == 1 else jnp.concatenate(parts, axis=0)
    pos = _iota(b.shape, 0) % (2 * h)
    if h == 1:
        return jnp.where(pos == 1, pltpu.roll(b, 1, 0), b)
    return jnp.where(pos == 0, pltpu.roll(b, c - 1, 0),
                     jnp.where(pos == 1, b,
                               jnp.where(pos == 2, pltpu.roll(b, 1, 0), pltpu.roll(b, 2, 0))))


def _gla_kernel(q_ref, k_ref, v_ref, g_ref, b_ref, ng_ref, o_ref, st_ref):
    @pl.when(pl.program_id(1) == 0)
    def _():
        st_ref[...] = jnp.zeros_like(st_ref)

    for ci in range(q_ref.shape[0] // CHUNK):
        rows = slice(ci * CHUNK, (ci + 1) * CHUNK)
        _gla_chunk(q_ref.at[rows], k_ref.at[rows], v_ref.at[rows], g_ref.at[rows], b_ref.at[rows],
                   ng_ref, o_ref.at[rows], st_ref)


def _gla_chunk(q_ref, k_ref, v_ref, g_ref, b_ref, ng_ref, o_ref, st_ref):
    c = CHUNK
    b_all = b_ref[...]
    row = _iota((c, c), 0)
    col = _iota((c, c), 1)
    eye = row == col

    heads = range(GLA_HEADS)
    ksl = [slice(hd * GLA_DK, (hd + 1) * GLA_DK) for hd in heads]
    vsl = [slice(hd * GLA_DV, (hd + 1) * GLA_DV) for hd in heads]
    q = q_ref[...].astype(F32) * (GLA_DK ** -0.5)
    k = k_ref[...].astype(F32)
    b = b_all
    b_last = b[c - 1:c, :]
    st = [st_ref[hd] for hd in heads]

    qg = (q * jnp.exp(b)).astype(BF16)
    kd = (k * jnp.exp(b_last - b)).astype(BF16)
    e_last = jnp.exp(b_last)
    o = [_dot_nt(qg[:, ksl[hd]], st[hd].astype(BF16)) for hd in heads]
    for hd in heads:
        vt = v_ref[:, vsl[hd]].astype(F32).T.astype(BF16)
        st_ref[hd] = st[hd] * e_last[:, ksl[hd]] + _dot(vt, kd[:, ksl[hd]])

    q16 = q.astype(BF16)
    k16 = k.astype(BF16)
    scores = [jnp.where(eye, _dot_nt(q16[:, ksl[hd]], k16[:, ksl[hd]]), 0.0) for hd in heads]
    h = c // 2
    while h >= 1:
        anchor = _level_anchor(b, h)
        second = (_iota(b.shape, 0) % (2 * h)) >= h
        decay = jnp.exp(jnp.where(second, b - anchor, anchor - b))
        qt = jnp.where(second, q * decay, 0.0).astype(BF16)
        kt = jnp.where(second, 0.0, k * decay).astype(BF16)
        same_block = (row // (2 * h)) == (col // (2 * h))
        for hd in heads:
            p = _dot_nt(qt[:, ksl[hd]], kt[:, ksl[hd]])
            if 2 * h < c:
                p = jnp.where(same_block, p, 0.0)
            scores[hd] = scores[hd] + p
        h //= 2

    for hd in heads:
        oh = o[hd] + _dot(scores[hd].astype(BF16), v_ref[:, vsl[hd]])
        g = g_ref[:, vsl[hd]].astype(F32)
        ms = jnp.mean(oh * oh, axis=-1, keepdims=True)
        o_ref[:, vsl[hd]] = (oh * lax.rsqrt(ms + EPS) * ng_ref[...] * _silu(g)).astype(o_ref.dtype)


def _gla(slab, b_all, norm_g, bsz, s_len):
    tr = RECUR_CHUNKS_PER_STEP * CHUNK
    nc = s_len // tr
    rb = lambda b, i: b * nc + i
    return pl.pallas_call(
        _gla_kernel,
        grid=(bsz, nc),
        in_specs=[
            pl.BlockSpec((tr, GLA_KEY_DIM), lambda b, i: (rb(b, i), COL_GLA_Q // GLA_KEY_DIM)),
            pl.BlockSpec((tr, GLA_KEY_DIM), lambda b, i: (rb(b, i), COL_GLA_K // GLA_KEY_DIM)),
            pl.BlockSpec((tr, GLA_VAL_DIM), lambda b, i: (rb(b, i), COL_GLA_V // GLA_VAL_DIM)),
            pl.BlockSpec((tr, GLA_VAL_DIM), lambda b, i: (rb(b, i), COL_GLA_G // GLA_VAL_DIM)),
            pl.BlockSpec((tr, GLA_KEY_DIM), lambda b, i: (rb(b, i), 0)),
            pl.BlockSpec((1, GLA_DV), lambda b, i: (0, 0)),
        ],
        out_specs=pl.BlockSpec((tr, GLA_VAL_DIM), lambda b, i: (rb(b, i), 0)),
        out_shape=jax.ShapeDtypeStruct((bsz * s_len, GLA_VAL_DIM), BF16),
        scratch_shapes=[pltpu.VMEM((GLA_HEADS, GLA_DV, GLA_DK), F32)],
        compiler_params=_cparams(("parallel", "arbitrary")),
        name="gla",
    )(slab, slab, slab, slab, b_all, norm_g)


def _ssd_kernel(xbc_ref, z_ref, dt_ref, acum_ref, cw_ref, cb_ref, dexp_ref, ng_ref,
                expand_ref, pairsel_ref, shift_ref, o_ref, xpad_ref, st_ref, yd_ref):
    pad = SSM_CONV_PAD
    tr = xbc_ref.shape[0]

    @pl.when(pl.program_id(1) == 0)
    def _():
        st_ref[...] = jnp.zeros_like(st_ref)
        xpad_ref[0:pad, :] = jnp.zeros((pad, SSM_CONV_DIM), BF16)

    xpad_ref[pad:pad + tr, :] = xbc_ref[...]
    for ci in range(tr // CHUNK):
        rows = slice(ci * CHUNK, (ci + 1) * CHUNK)
        _ssd_chunk(xpad_ref.at[ci * CHUNK:(ci + 1) * CHUNK + pad], xbc_ref.at[rows], z_ref.at[rows],
                   dt_ref.at[rows], acum_ref.at[rows], cw_ref, cb_ref, dexp_ref, ng_ref,
                   expand_ref, pairsel_ref, shift_ref, o_ref.at[rows], st_ref, yd_ref.at[rows])
    xpad_ref[0:pad, :] = xpad_ref[tr:tr + pad, :]


def _ssd_chunk(xwin_ref, xbc_ref, z_ref, dt_ref, acum_ref, cw_ref, cb_ref, dexp_ref, ng_ref,
               expand_ref, pairsel_ref, shift_ref, o_ref, st_ref, yd_ref):
    c = CHUNK
    n = SSM_STATE
    gw = SSM_GROUP_WIDTH

    delayed = _dot(shift_ref[...], xwin_ref[...])
    conv = cb_ref[...] + xbc_ref[...].astype(F32) * cw_ref[SSM_CONV_W - 1:SSM_CONV_W, :]
    for i in range(SSM_CONV_W - 1):
        conv = conv + delayed[i * c:(i + 1) * c] * cw_ref[i:i + 1, :]
    xc = _silu(conv)
    xs = xc[:, :SSM_D_INNER]

    a_cum = acum_ref[...]
    both = _dot_sel_rhs2(jnp.concatenate([dt_ref[...], a_cum], axis=0), expand_ref[...])
    dt_e = both[:c]
    a_e = both[c:]
    a_last_e = a_e[c - 1:c, :]
    xdt = xs * dt_e
    decay_out = jnp.exp(a_e)
    xdtd = (xdt * jnp.exp(a_last_e - a_e)).astype(BF16)
    chunk_decay = jnp.exp(a_last_e)

    lane = _iota((c, LANES), 1)
    z0 = jnp.concatenate([jnp.where(lane % 2 == 0, a_cum, 0.0),
                          jnp.where(lane % 2 == 1, a_cum, 0.0)], axis=0)
    pairs = _dot_nt_sel_lhs2(pairsel_ref[...], z0)

    lane2 = _iota((c, 2 * c), 1)
    row2 = _iota((c, 2 * c), 0)
    causal2 = (lane2 % c) <= row2
    bd_row = _iota((2 * c, 2 * c), 0)
    bd_col = _iota((2 * c, 2 * c), 1)
    bd_mask = (bd_row // c) == (bd_col // c)

    for g in range(SSM_GROUPS):
        bg = xc[:, SSM_D_INNER + g * n:SSM_D_INNER + (g + 1) * n]
        cg = xc[:, SSM_D_INNER + SSM_BC_WIDTH + g * n:SSM_D_INNER + SSM_BC_WIDTH + (g + 1) * n]
        cg16 = cg.astype(BF16)
        bg16 = bg.astype(BF16)
        gsl = slice(g * gw, (g + 1) * gw)
        st = st_ref[g]

        y_off = _dot(cg16, st.astype(BF16)) * decay_out[:, gsl]
        st_ref[g] = st * chunk_decay[:, gsl] + _dot(bg.T.astype(BF16), xdtd[:, gsl])

        cb2 = _dot_nt(cg16, jnp.concatenate([bg16, bg16], axis=0))
        for pr in range(SSM_HEADS_PER_GROUP // 2):
            pidx = g * (SSM_HEADS_PER_GROUP // 2) + pr
            lsl = slice(pidx * 2 * c, (pidx + 1) * 2 * c)
            seg = a_e[:, lsl] - pairs[pidx:pidx + 1, :]
            lmat = jnp.exp(jnp.where(causal2, seg, MASK_VALUE))
            w = (cb2 * lmat).astype(BF16)
            xp = xdt[:, lsl]
            bd = jnp.where(bd_mask, jnp.concatenate([xp, xp], axis=0), 0.0).astype(BF16)
            yd_ref[:, lsl] = _dot(w, bd)

        y = yd_ref[:, gsl] + y_off + xs[:, gsl] * dexp_ref[:, gsl]
        y = y * _silu(z_ref[:, gsl].astype(F32))
        ms = jnp.mean(y * y, axis=-1, keepdims=True)
        o_ref[:, gsl] = (y * lax.rsqrt(ms + EPS) * ng_ref[:, gsl]).astype(o_ref.dtype)


def _ssd(slab, dt, a_cum, conv_w, conv_b, d_exp, norm_g, expand, pairsel, shift, bsz, s_len):
    tr = RECUR_CHUNKS_PER_STEP * CHUNK
    nc = s_len // tr
    rb = lambda b, i: b * nc + i
    const = lambda b, i: (0, 0)
    return pl.pallas_call(
        _ssd_kernel,
        grid=(bsz, nc),
        in_specs=[
            pl.BlockSpec((tr, SSM_CONV_DIM), lambda b, i: (rb(b, i), COL_XBC // SSM_CONV_DIM)),
            pl.BlockSpec((tr, SSM_D_INNER), lambda b, i: (rb(b, i), COL_SSM_Z // SSM_D_INNER)),
            pl.BlockSpec((tr, LANES), lambda b, i: (rb(b, i), 0)),
            pl.BlockSpec((tr, LANES), lambda b, i: (rb(b, i), 0)),
            pl.BlockSpec((SSM_CONV_W, SSM_CONV_DIM), const),
            pl.BlockSpec((1, SSM_CONV_DIM), const),
            pl.BlockSpec((1, SSM_D_INNER), const),
            pl.BlockSpec((1, SSM_D_INNER), const),
            pl.BlockSpec((LANES, SSM_D_INNER), const),
            pl.BlockSpec((SSM_HEADS // 2, LANES), const),
            pl.BlockSpec(((SSM_CONV_W - 1) * CHUNK, CHUNK + SSM_CONV_PAD), const),
        ],
        out_specs=pl.BlockSpec((tr, SSM_D_INNER), lambda b, i: (rb(b, i), 0)),
        out_shape=jax.ShapeDtypeStruct((bsz * s_len, SSM_D_INNER), BF16),
        scratch_shapes=[
            pltpu.VMEM((tr + SSM_CONV_PAD, SSM_CONV_DIM), BF16),
            pltpu.VMEM((SSM_GROUPS, SSM_STATE, SSM_GROUP_WIDTH), F32),
            pltpu.VMEM((tr, SSM_D_INNER), F32),
        ],
        compiler_params=_cparams(("parallel", "arbitrary")),
        name="ssd",
    )(slab, slab, dt, a_cum, conv_w, conv_b, d_exp, norm_g, expand, pairsel, shift)


def _rope_table_kernel(pos_ref, freq_ref, c_ref, s1_ref, s2_ref):
    ang = pos_ref[...].astype(F32) * freq_ref[...]
    d = _iota(ang.shape, 1) % DIFF_HEAD_DIM
    half = ROT_DIM // 2
    cosv = jnp.cos(ang)
    sinv = jnp.sin(ang)
    c_ref[...] = jnp.where(d < ROT_DIM, cosv, 1.0)
    s1_ref[...] = jnp.where(d < half, -sinv, 0.0)
    s2_ref[...] = jnp.where((d >= half) & (d < ROT_DIM), sinv, 0.0)


def _rope_tables(pos_col, freq_row, tr=512):
    m = pos_col.shape[0]
    spec = pl.BlockSpec((tr, LANES), lambda i: (i, 0))
    shp = jax.ShapeDtypeStruct((m, LANES), F32)
    return pl.pallas_call(
        _rope_table_kernel,
        grid=(m // tr,),
        in_specs=[pl.BlockSpec((tr, 1), lambda i: (i, 0)), pl.BlockSpec((1, LANES), lambda i: (0, 0))],
        out_specs=[spec, spec, spec],
        out_shape=[shp, shp, shp],
        compiler_params=_cparams(("parallel",)),
        name="rope_tables",
    )(pos_col, freq_row)


def _attn_prep_kernel(q_ref, k_ref, v_ref, c_ref, s1_ref, s2_ref, qt_ref, ko_ref, vt_ref):
    half = ROT_DIM // 2
    cosf = c_ref[...]
    s1 = s1_ref[...]
    s2 = s2_ref[...]

    tile_rows = DIFF_V_ROWS - DIFF_V_DIM
    ones_tile = jnp.where(_iota((tile_rows, q_ref.shape[0]), 0) == 0, 1.0, 0.0).astype(BF16)

    def rope(t):
        return t * cosf + pltpu.roll(t, LANES - half, 1) * s1 + pltpu.roll(t, half, 1) * s2

    for hd in range(DIFF_HEADS):
        sl = slice(hd * LANES, (hd + 1) * LANES)
        q = q_ref[:, sl].astype(F32)
        k = k_ref[:, sl].astype(F32)
        qt_ref[0, hd, 0] = (rope(q) * QK_SCALE_LOG2).T.astype(BF16)
        ko_ref[:, sl] = rope(k).astype(BF16)
        vt_ref[0, hd, 0, 0:DIFF_V_DIM, :] = v_ref[:, sl].astype(F32).T.astype(BF16)
        vt_ref[0, hd, 0, DIFF_V_DIM:DIFF_V_ROWS, :] = ones_tile


def _attn_prep(slab, cosf, s1, s2, bsz, s_len, tr):
    m = bsz * s_len
    nb = s_len // tr
    w = DIFF_QK_WIDTH
    row = pl.BlockSpec((tr, w), lambda i: (i, 0))
    tab = pl.BlockSpec((tr, LANES), lambda i: (i, 0))
    tposed = lambda rows: pl.BlockSpec((1, DIFF_HEADS, 1, rows, tr), lambda i: (i // nb, 0, i % nb, 0, 0))
    return pl.pallas_call(
        _attn_prep_kernel,
        grid=(m // tr,),
        in_specs=[
            pl.BlockSpec((tr, w), lambda i: (i, COL_DIFF_Q // w)),
            pl.BlockSpec((tr, w), lambda i: (i, COL_DIFF_K // w)),
            pl.BlockSpec((tr, w), lambda i: (i, COL_DIFF_V // w)),
            tab, tab, tab,
        ],
        out_specs=[tposed(LANES), row, tposed(DIFF_V_ROWS)],
        out_shape=[
            jax.ShapeDtypeStruct((bsz, DIFF_HEADS, nb, LANES, tr), BF16),
            jax.ShapeDtypeStruct((m, w), BF16),
            jax.ShapeDtypeStruct((bsz, DIFF_HEADS, nb, DIFF_V_ROWS, tr), BF16),
        ],
        compiler_params=_cparams(("parallel",)),
        name="attn_prep",
    )(slab, slab, slab, cosf, s1, s2)


def _flash_kernel(q_ref, qn_ref, k_ref, vt_ref, lq1_ref, lk1_ref, lq2_ref, lk2_ref, ng_ref, o_ref,
                  sa_ref, sb_ref, sc_ref, *, tq, cw, lambda_init):
    qi = pl.program_id(2)
    nch = 2 * tq // cw
    lead = 1

    def chains(ref):
        qt = ref[0, 0, 0]
        feat = _iota(qt.shape, 0)
        zero = jnp.zeros_like(qt)
        qs = jnp.concatenate([jnp.where(feat < DIFF_HEAD_DIM, qt, zero),
                              jnp.where(feat >= DIFF_HEAD_DIM, qt, zero)], axis=1)
        return [qs[:, c * cw:(c + 1) * cw] for c in range(nch)]

    q_chains = chains(q_ref)

    def keys(j):
        return k_ref[pl.ds(pl.multiple_of(j * tq, tq), tq), :]

    def scores(j, s_ref, qc):
        kj = keys(j)
        for c in range(nch):
            s_ref[c] = _dot(kj, qc[c])

    def absorb(j, s_ref, stats, masked, nxt_ref=None):
        vtj = vt_ref[0, 0, j]
        kn = None if nxt_ref is None else keys(j + 1)
        out = []
        if nxt_ref is not None:
            for c in range(min(lead, nch)):
                nxt_ref[c] = _dot(kn, q_chains[c])
        for c in range(nch):
            if nxt_ref is not None and c + lead < nch:
                nxt_ref[c + lead] = _dot(kn, q_chains[c + lead])
            m, acc = stats[c]
            if masked:
                q0 = (c * cw) % tq
                nk = q0 + cw
                kc = _iota((nk // CHUNK, 1, cw), 0)
                qc = (q0 + _iota((nk // CHUNK, 1, cw), 2)) // CHUNK
                bias = jnp.where(kc <= qc, 0.0, MASK_VALUE)
                s = (s_ref[c, 0:nk, :].reshape(nk // CHUNK, CHUNK, cw) + bias).reshape(nk, cw)
                vt = vtj[:, 0:nk]
            else:
                s = s_ref[c]
                vt = vtj
            m_new = jnp.maximum(m, jnp.max(s, axis=0, keepdims=True))
            p = jnp.exp2(s - m_new).astype(BF16)
            alpha = jnp.exp2(m - m_new)
            acc = acc * alpha + _dot(vt, p)
            out.append((m_new, acc))
        return tuple(out)

    def score_next_block():
        scores(0, sc_ref, chains(qn_ref))

    def pair(j, stats):
        stats = absorb(j, sa_ref, stats, False, sb_ref)
        return absorb(j + 1, sb_ref, stats, False, sa_ref)

    def quad(jj, stats):
        return pair(4 * jj + 3, pair(4 * jj + 1, stats))

    def last_even(stats):
        score_next_block()
        return absorb(qi, sa_ref, stats, True)

    def last_odd(stats):
        stats = absorb(qi - 1, sa_ref, stats, False, sb_ref)
        score_next_block()
        return absorb(qi, sb_ref, stats, True)

    init = tuple((jnp.full((1, cw), MASK_VALUE, F32), jnp.zeros((DIFF_V_ROWS, cw), F32))
                 for _ in range(nch))

    def first_block(stats):
        scores(0, sc_ref, q_chains)
        stats = absorb(0, sc_ref, stats, True)
        score_next_block()
        return stats

    def later_block(stats):
        rest = qi - 1
        stats = absorb(0, sc_ref, stats, False, sa_ref)
        stats = lax.fori_loop(0, rest // 4, quad, stats)
        stats = lax.cond(rest % 4 >= 2, lambda st: pair(4 * (rest // 4) + 1, st), lambda st: st, stats)
        return lax.cond(rest % 2 == 1, last_odd, last_even, stats)

    carry = lax.cond(qi == 0, first_block, later_block, init)

    lam = (jnp.exp(jnp.sum(lq1_ref[...] * lk1_ref[...], keepdims=True))
           - jnp.exp(jnp.sum(lq2_ref[...] * lk2_ref[...], keepdims=True)) + lambda_init)
    on = jnp.concatenate([acc[0:DIFF_V_DIM] / acc[DIFF_V_DIM:DIFF_V_DIM + 1] for (_, acc) in carry],
                         axis=1)
    o = on[:, :tq] - lam * on[:, tq:]
    ms = jnp.mean(o * o, axis=0, keepdims=True)
    o = o * lax.rsqrt(ms + DIFF_SUBLN_EPS)
    o_ref[...] = (o.T * ng_ref[...] * (1.0 - lambda_init)).astype(o_ref.dtype)


def _flash(q_t, k_r, v_t, lq1, lk1, lq2, lk2, norm_g, bsz, s_len, tq, lambda_init):
    nq = s_len // tq
    cw = min(256, tq)
    vec = pl.BlockSpec((1, DIFF_HEAD_DIM), lambda b, h, i: (0, 0))
    return pl.pallas_call(
        functools.partial(_flash_kernel, tq=tq, cw=cw, lambda_init=lambda_init),
        grid=(bsz, DIFF_HEADS, nq),
        in_specs=[
            pl.BlockSpec((1, 1, 1, LANES, tq), lambda b, h, i: (b, h, i, 0, 0)),
            pl.BlockSpec((1, 1, 1, LANES, tq), lambda b, h, i: (b, h, jnp.minimum(i + 1, nq - 1), 0, 0)),
            pl.BlockSpec((s_len, LANES), lambda b, h, i: (b, h)),
            pl.BlockSpec((1, 1, nq, DIFF_V_ROWS, tq), lambda b, h, i: (b, h, 0, 0, 0)),
            vec, vec, vec, vec,
            pl.BlockSpec((1, DIFF_V_DIM), lambda b, h, i: (0, 0)),
        ],
        out_specs=pl.BlockSpec((tq, DIFF_V_DIM), lambda b, h, i: (b * nq + i, h)),
        out_shape=jax.ShapeDtypeStruct((bsz * s_len, DIFF_V_WIDTH), BF16),
        scratch_shapes=[pltpu.VMEM((2 * tq // cw, tq, cw), F32)] * 3,
        compiler_params=_cparams(("parallel", "parallel", "arbitrary")),
        name="flash",
    )(q_t, q_t, k_r, v_t, lq1, lk1, lq2, lk2, norm_g)


def _merge_kernel(x_ref, yg_ref, ys_ref, yd_ref, g0_ref, g1_ref, g2_ref, bg_ref,
                  wg_ref, ws_ref, wd_ref, wo_ref, o_ref):
    def gate(g_ref, i):
        return 1.0 / (1.0 + jnp.exp(-(g_ref[...].astype(F32) + bg_ref[i:i + 1, :])))

    mixed = gate(g0_ref, 0) * _dot(yg_ref[...], wg_ref[...])
    mixed = mixed + gate(g1_ref, 1) * _dot(ys_ref[...], ws_ref[...])
    mixed = mixed + gate(g2_ref, 2) * _dot(yd_ref[...], wd_ref[...])
    o_ref[...] = x_ref[...] + _dot(mixed.astype(BF16), wo_ref[...])


def _merge(x2, y_gla, y_ssm, y_diff, slab, b_gate, w_g, w_s, w_d, w_o, layer, tm=512):
    m = x2.shape[0]
    gcol = COL_GATE // D_MODEL
    row = lambda w: pl.BlockSpec((tm, w), lambda i: (i, 0))
    const = lambda r, c: pl.BlockSpec((r, c), lambda i: (0, 0))
    weight = lambda r, c: pl.BlockSpec((None, r, c), lambda i: (layer, 0, 0))
    return pl.pallas_call(
        _merge_kernel,
        grid=(m // tm,),
        in_specs=[
            row(D_MODEL), row(GLA_VAL_DIM), row(SSM_D_INNER), row(DIFF_V_WIDTH),
            pl.BlockSpec((tm, D_MODEL), lambda i: (i, gcol)),
            pl.BlockSpec((tm, D_MODEL), lambda i: (i, gcol + 1)),
            pl.BlockSpec((tm, D_MODEL), lambda i: (i, gcol + 2)),
            const(N_BRANCHES, D_MODEL),
            weight(GLA_VAL_DIM, D_MODEL), weight(SSM_D_INNER, D_MODEL), weight(DIFF_V_WIDTH, D_MODEL),
            weight(D_MODEL, D_MODEL),
        ],
        out_specs=row(D_MODEL),
        out_shape=jax.ShapeDtypeStruct((m, D_MODEL), F32),
        compiler_params=_cparams(("parallel",)),
        name="merge",
    )(x2, y_gla, y_ssm, y_diff, slab, slab, slab, b_gate, w_g, w_s, w_d, w_o)


def _mlp_kernel(x_ref, g_ref, wu_ref, wd_ref, gf_ref, o_ref, *, final_norm, fc):
    x = x_ref[...]
    ms = jnp.mean(x * x, axis=-1, keepdims=True)
    h = (x * lax.rsqrt(ms + EPS) * g_ref[...]).astype(BF16)
    acc = x
    for c0 in range(0, D_FF, fc):
        u = jnp.maximum(_dot(h, wu_ref[:, c0:c0 + fc]), 0.0)
        acc = acc + _dot((u * u).astype(BF16), wd_ref[c0:c0 + fc, :])
    if final_norm:
        ms = jnp.mean(acc * acc, axis=-1, keepdims=True)
        acc = acc * lax.rsqrt(ms + EPS) * gf_ref[...]
    o_ref[...] = acc


def _mlp(x2, g, w_up, w_down, g_final, final_norm, layer, tm=512, fc=1024):
    m = x2.shape[0]
    row = pl.BlockSpec((tm, D_MODEL), lambda i: (i, 0))
    const = lambda r, c: pl.BlockSpec((r, c), lambda i: (0, 0))
    weight = lambda r, c: pl.BlockSpec((None, r, c), lambda i: (layer, 0, 0))
    return pl.pallas_call(
        functools.partial(_mlp_kernel, final_norm=final_norm, fc=fc),
        grid=(m // tm,),
        in_specs=[row, const(1, D_MODEL), weight(D_MODEL, D_FF), weight(D_FF, D_MODEL), const(1, D_MODEL)],
        out_specs=row,
        out_shape=jax.ShapeDtypeStruct((m, D_MODEL), F32),
        compiler_params=_cparams(("parallel",)),
        name="mlp",
    )(x2, g, w_up, w_down, g_final)


def _slab_weights(w_in):
    sizes = (GLA_KEY_DIM, GLA_KEY_DIM, GLA_VAL_DIM, GLA_LOW_RANK, GLA_VAL_DIM,
             SSM_D_INNER, SSM_CONV_DIM, SSM_HEADS,
             DIFF_QK_WIDTH, DIFF_QK_WIDTH, DIFF_V_WIDTH, N_BRANCHES * D_MODEL)
    offs = np.concatenate([[0], np.cumsum(sizes)])
    (a_q, a_k, a_v, a_gk, a_g, b_z, b_xbc, b_dt, c_q, c_k, c_v, gate) = [
        w_in[:, :, int(offs[i]):int(offs[i + 1])] for i in range(len(sizes))]
    slab = jnp.concatenate([b_xbc, gate, a_v, a_g, c_q, c_k, c_v, a_q, a_k, b_z], axis=2).astype(BF16)
    zeros = lambda n: jnp.zeros((w_in.shape[0], D_MODEL, n), w_in.dtype)
    small = jnp.concatenate([a_gk, zeros(LANES - GLA_LOW_RANK), b_dt, zeros(LANES - SSM_HEADS)],
                            axis=2).astype(BF16)
    return slab, small


def _pad_lanes(v, fill=0.0):
    v = v.reshape(1, -1).astype(F32)
    return jnp.pad(v, ((0, 0), (0, LANES - v.shape[1])), constant_values=fill)


def _forward(x, positions, norm_mix_g, w_in, b_gate, gla_w_gk2, gla_b_gk, gla_norm_g,
             ssm_conv_w, ssm_conv_b, ssm_dt_bias, ssm_a_log, ssm_d, ssm_norm_g,
             diff_lq1, diff_lk1, diff_lq2, diff_lk2, diff_norm_g,
             w_br_gla, w_br_ssm, w_br_diff, w_out, norm_mlp_g, w_mlp_up, w_mlp_down,
             norm_final_g, tq):
    bsz, s_len, _ = x.shape
    m = bsz * s_len
    depth = w_in.shape[0]
    x2 = x.reshape(m, D_MODEL).astype(F32)

    head_of_lane = np.arange(SSM_D_INNER) // SSM_HEAD_DIM
    expand = jnp.asarray((np.arange(LANES)[:, None] == head_of_lane[None, :]), BF16)
    pairsel = jnp.asarray((np.arange(LANES)[None, :] // 2) == np.arange(SSM_HEADS // 2)[:, None], BF16)
    pairsel = pairsel * jnp.asarray(np.arange(LANES)[None, :] < SSM_HEADS, BF16)
    t = np.arange((SSM_CONV_W - 1) * CHUNK)
    src_row = SSM_CONV_PAD + (t % CHUNK) - (SSM_CONV_W - 1) + (t // CHUNK)
    shift = jnp.asarray(np.arange(CHUNK + SSM_CONV_PAD)[None, :] == src_row[:, None], BF16)
    d = np.arange(LANES) % DIFF_HEAD_DIM
    inv_freq = ROPE_THETA ** (-(2.0 * (d % (ROT_DIM // 2))) / ROT_DIM)
    freq_row = jnp.asarray(np.where(d < ROT_DIM, inv_freq, 0.0).reshape(1, LANES), F32)

    cosf, s1, s2 = _rope_tables(positions.reshape(m, 1), freq_row, tr=min(512, s_len))

    w_slab, w_small = _slab_weights(w_in)
    wb_gla, wb_ssm, wb_diff, wb_out = (w.astype(BF16) for w in (w_br_gla, w_br_ssm, w_br_diff, w_out))
    wb_up, wb_down = w_mlp_up.astype(BF16), w_mlp_down.astype(BF16)

    for l in range(depth):
        slab, small = _inproj(x2, norm_mix_g[l].reshape(1, -1), w_slab, w_small, l,
                              tm=min(1024, m))

        wgk_pad = jnp.pad(gla_w_gk2[l].astype(F32), ((0, LANES - GLA_LOW_RANK), (0, 0)))
        b_all, dt, a_cum = _gates(small, wgk_pad, gla_b_gk[l].reshape(1, -1),
                                  _pad_lanes(ssm_dt_bias[l]), _pad_lanes(ssm_a_log[l]))
        y_gla = _gla(slab, b_all, gla_norm_g[l].reshape(1, -1), bsz, s_len)

        d_exp = jnp.repeat(ssm_d[l].astype(F32), SSM_HEAD_DIM).reshape(1, -1)
        y_ssm = _ssd(slab, dt, a_cum, ssm_conv_w[l], ssm_conv_b[l].reshape(1, -1), d_exp,
                     ssm_norm_g[l].reshape(1, -1), expand, pairsel, shift, bsz, s_len)

        lambda_init = 0.8 - 0.6 * math.exp(-0.3 * l)
        q_t, k_r, v_t = _attn_prep(slab, cosf, s1, s2, bsz, s_len, tq)
        y_diff = _flash(q_t, k_r, v_t, diff_lq1[l].reshape(1, -1), diff_lk1[l].reshape(1, -1),
                        diff_lq2[l].reshape(1, -1), diff_lk2[l].reshape(1, -1),
                        diff_norm_g[l].reshape(1, -1), bsz, s_len, tq, lambda_init)

        x2 = _merge(x2, y_gla, y_ssm, y_diff, slab, b_gate[l].reshape(N_BRANCHES, D_MODEL),
                    wb_gla, wb_ssm, wb_diff, wb_out, l, tm=min(512, m))
        x2 = _mlp(x2, norm_mlp_g[l].reshape(1, -1), wb_up, wb_down, norm_final_g.reshape(1, -1),
                  final_norm=(l == depth - 1), layer=l, tm=min(512, m))
    return x2.reshape(bsz, s_len, D_MODEL)


def kernel(x, positions, norm_mix_g, w_in, b_gate, gla_w_gk2, gla_b_gk, gla_norm_g, ssm_conv_w,
           ssm_conv_b, ssm_dt_bias, ssm_a_log, ssm_d, ssm_norm_g, diff_lq1, diff_lk1, diff_lq2,
           diff_lk2, diff_norm_g, w_br_gla, w_br_ssm, w_br_diff, w_out, norm_mlp_g, w_mlp_up,
           w_mlp_down, norm_final_g):
    return _forward(x, positions, norm_mix_g, w_in, b_gate, gla_w_gk2, gla_b_gk, gla_norm_g,
                    ssm_conv_w, ssm_conv_b, ssm_dt_bias, ssm_a_log, ssm_d, ssm_norm_g,
                    diff_lq1, diff_lk1, diff_lq2, diff_lk2, diff_norm_g,
                    w_br_gla, w_br_ssm, w_br_diff, w_out, norm_mlp_g, w_mlp_up, w_mlp_down,
                    norm_final_g, tq=min(512, x.shape[1]))
```

```python
import functools
import math

import jax
import jax.numpy as jnp
import numpy as np
from jax import lax
from jax.experimental import pallas as pl
from jax.experimental.pallas import tpu as pltpu

F32 = jnp.float32
BF16 = jnp.bfloat16

D_MODEL = 1024
DEPTH = 2
CHUNK = 64
ROPE_THETA = 500000.0
EPS = 1e-6

GLA_HEADS = 4
GLA_KEY_DIM = D_MODEL // 2
GLA_VAL_DIM = D_MODEL
GLA_DK = GLA_KEY_DIM // GLA_HEADS
GLA_DV = GLA_VAL_DIM // GLA_HEADS
GLA_LOW_RANK = 16
GLA_GATE_NORMALIZER = 16.0

SSM_D_INNER = 2 * D_MODEL
SSM_HEAD_DIM = 64
SSM_HEADS = SSM_D_INNER // SSM_HEAD_DIM
SSM_GROUPS = 4
SSM_HEADS_PER_GROUP = SSM_HEADS // SSM_GROUPS
SSM_STATE = 128
SSM_CONV_W = 4
SSM_BC_WIDTH = SSM_GROUPS * SSM_STATE
SSM_CONV_DIM = SSM_D_INNER + 2 * SSM_BC_WIDTH
SSM_GROUP_WIDTH = SSM_D_INNER // SSM_GROUPS
SSM_CONV_PAD = 16
GLA_CHUNKS_PER_STEP = 8
SSD_CHUNKS_PER_STEP = 4

DIFF_HEADS = 8
DIFF_HEAD_DIM = 64
DIFF_V_DIM = 2 * DIFF_HEAD_DIM
DIFF_QK_WIDTH = DIFF_HEADS * 2 * DIFF_HEAD_DIM
DIFF_V_WIDTH = DIFF_HEADS * DIFF_V_DIM
ROT_DIM = DIFF_HEAD_DIM // 4
DIFF_SUBLN_EPS = 1e-5
DIFF_V_ROWS = DIFF_V_DIM + 16

N_BRANCHES = 3
D_FF = 4 * D_MODEL

LANES = 128
MASK_VALUE = -1e30
QK_SCALE_LOG2 = DIFF_HEAD_DIM ** -0.5 * math.log2(math.e)

COL_XBC = 0
COL_GATE = COL_XBC + SSM_CONV_DIM
COL_GLA_V = COL_GATE + N_BRANCHES * D_MODEL
COL_GLA_G = COL_GLA_V + GLA_VAL_DIM
COL_DIFF_Q = COL_GLA_G + GLA_VAL_DIM
COL_DIFF_K = COL_DIFF_Q + DIFF_QK_WIDTH
COL_DIFF_V = COL_DIFF_K + DIFF_QK_WIDTH
COL_GLA_Q = COL_DIFF_V + DIFF_V_WIDTH
COL_GLA_K = COL_GLA_Q + GLA_KEY_DIM
COL_SSM_Z = COL_GLA_K + GLA_KEY_DIM
SLAB_COLS = COL_SSM_Z + SSM_D_INNER
SMALL_COLS = 2 * LANES

VMEM_LIMIT = 56 * 1024 * 1024


def _cparams(sem):
    return pltpu.CompilerParams(dimension_semantics=sem, vmem_limit_bytes=VMEM_LIMIT)


def _dot(a, b):
    return jnp.dot(a, b, preferred_element_type=F32)


def _dot_nt(a, b):
    return lax.dot_general(a, b, (((1,), (1,)), ((), ())), preferred_element_type=F32)


def _split3(x):
    hi = x.astype(BF16)
    r1 = x - hi.astype(F32)
    mid = r1.astype(BF16)
    lo = (r1 - mid.astype(F32)).astype(BF16)
    return hi, mid, lo


def _split2(x):
    hi = x.astype(BF16)
    return hi, (x - hi.astype(F32)).astype(BF16)


def _dot_sel_lhs2(sel, x):
    hi, mid = _split2(x)
    return _dot(sel, hi) + _dot(sel, mid)


def _dot_sel_rhs2(x, sel):
    hi, mid = _split2(x)
    return _dot(hi, sel) + _dot(mid, sel)


def _dot_nt_sel_lhs2(sel, x):
    hi, mid = _split2(x)
    return _dot_nt(sel, hi) + _dot_nt(sel, mid)


def _dot_f32(a, b):
    ah, am, _ = _split3(a)
    bh, bm, _ = _split3(b)
    return _dot(ah, bh) + _dot(ah, bm) + _dot(am, bh)


def _silu(x):
    return x / (1.0 + jnp.exp(-x))


def _log1p_exp_neg_abs(x):
    return jnp.log(1.0 + jnp.exp(-jnp.abs(x)))


def _log_sigmoid(x):
    return jnp.minimum(x, 0.0) - _log1p_exp_neg_abs(x)


def _softplus(x):
    return jnp.maximum(x, 0.0) + _log1p_exp_neg_abs(x)


def _iota(shape, dim):
    return lax.broadcasted_iota(jnp.int32, shape, dim)


def _inproj_kernel(x_ref, g_ref, w_ref, ws_ref, o_ref, os_ref, h_ref):
    @pl.when(pl.program_id(1) == 0)
    def _():
        x = x_ref[...]
        ms = jnp.mean(x * x, axis=-1, keepdims=True)
        h = (x * lax.rsqrt(ms + EPS) * g_ref[...]).astype(BF16)
        h_ref[...] = h
        os_ref[...] = _dot(h, ws_ref[...])

    o_ref[...] = _dot(h_ref[...], w_ref[...]).astype(o_ref.dtype)


def _inproj(x2, g, w_slab, w_small, layer, tm=1024, tn=2048):
    m = x2.shape[0]
    return pl.pallas_call(
        _inproj_kernel,
        grid=(m // tm, SLAB_COLS // tn),
        in_specs=[
            pl.BlockSpec((tm, D_MODEL), lambda i, j: (i, 0)),
            pl.BlockSpec((1, D_MODEL), lambda i, j: (0, 0)),
            pl.BlockSpec((None, D_MODEL, tn), lambda i, j: (layer, 0, j)),
            pl.BlockSpec((None, D_MODEL, SMALL_COLS), lambda i, j: (layer, 0, 0)),
        ],
        out_specs=[
            pl.BlockSpec((tm, tn), lambda i, j: (i, j)),
            pl.BlockSpec((tm, SMALL_COLS), lambda i, j: (i, 0)),
        ],
        out_shape=[
            jax.ShapeDtypeStruct((m, SLAB_COLS), BF16),
            jax.ShapeDtypeStruct((m, SMALL_COLS), F32),
        ],
        scratch_shapes=[pltpu.VMEM((tm, D_MODEL), BF16)],
        compiler_params=_cparams(("parallel", "arbitrary")),
        name="inproj",
    )(x2, g, w_slab, w_small)


def _gates_kernel(sm_ref, wgk_ref, bgk_ref, dtb_ref, alog_ref, b_ref, dt_ref, acum_ref):
    rows = sm_ref.shape[0]
    r = _iota((rows, rows), 0)
    c = _iota((rows, rows), 1)
    tril = jnp.where((c <= r) & ((r // CHUNK) == (c // CHUNK)), 1.0, 0.0).astype(BF16)

    gk = _log_sigmoid(_dot_f32(sm_ref[:, :LANES], wgk_ref[...]) + bgk_ref[...]) / GLA_GATE_NORMALIZER
    b_ref[...] = _dot_sel_lhs2(tril, gk)

    dt = _softplus(sm_ref[:, LANES:] + dtb_ref[...])
    dt_ref[...] = dt
    acum_ref[...] = _dot_sel_lhs2(tril, dt * (-jnp.exp(alog_ref[...])))


def _gates(small, wgk_pad, b_gk, dtb_pad, alog_pad, rows=256):
    m = small.shape[0]
    rows = min(rows, m)
    const = lambda r, c: pl.BlockSpec((r, c), lambda i: (0, 0))
    out = lambda w: pl.BlockSpec((rows, w), lambda i: (i, 0))
    return pl.pallas_call(
        _gates_kernel,
        grid=(m // rows,),
        in_specs=[pl.BlockSpec((rows, SMALL_COLS), lambda i: (i, 0)),
                  const(LANES, GLA_KEY_DIM), const(1, GLA_KEY_DIM), const(1, LANES), const(1, LANES)],
        out_specs=[out(GLA_KEY_DIM), out(LANES), out(LANES)],
        out_shape=[jax.ShapeDtypeStruct((m, GLA_KEY_DIM), F32),
                   jax.ShapeDtypeStruct((m, LANES), F32),
                   jax.ShapeDtypeStruct((m, LANES), F32)],
        compiler_params=_cparams(("parallel",)),
        name="gates",
    )(small, wgk_pad, b_gk, dtb_pad, alog_pad)


def _level_anchor(b, h):
    c = CHUNK
    if 2 * h >= 8:
        parts = [jnp.broadcast_to(b[s + h - 1:s + h, :], (2 * h, b.shape[1]))
                 for s in range(0, c, 2 * h)]
        return parts[0] if len(parts) == 1 else jnp.concatenate(parts, axis=0)
    pos = _iota(b.shape, 0) % (2 * h)
    if h == 1:
        return jnp.where(pos == 1, pltpu.roll(b, 1, 0), b)
    return jnp.where(pos == 0, pltpu.roll(b, c - 1, 0),
                     jnp.where(pos == 1, b,
                               jnp.where(pos == 2, pltpu.roll(b, 1, 0), pltpu.roll(b, 2, 0))))


def _gla_kernel(q_ref, k_ref, v_ref, g_ref, b_ref, ng_ref, o_ref, st_ref):
    @pl.when(pl.program_id(1) == 0)
    def _():
        st_ref[...] = jnp.zeros_like(st_ref)

    for ci in range(q_ref.shape[0] // CHUNK):
        rows = slice(ci * CHUNK, (ci + 1) * CHUNK)
        _gla_chunk(q_ref.at[rows], k_ref.at[rows], v_ref.at[rows], g_ref.at[rows], b_ref.at[rows],
                   ng_ref, o_ref.at[rows], st_ref)


def _gla_chunk(q_ref, k_ref, v_ref, g_ref, b_ref, ng_ref, o_ref, st_ref):
    c = CHUNK
    b_all = b_ref[...]
    row = _iota((c, c), 0)
    col = _iota((c, c), 1)
    eye = row == col

    heads = range(GLA_HEADS)
    ksl = [slice(hd * GLA_DK, (hd + 1) * GLA_DK) for hd in heads]
    vsl = [slice(hd * GLA_DV, (hd + 1) * GLA_DV) for hd in heads]
    q = q_ref[...].astype(F32) * (GLA_DK ** -0.5)
    k = k_ref[...].astype(F32)
    b = b_all
    b_last = b[c - 1:c, :]
    st = [st_ref[hd] for hd in heads]

    qg = (q * jnp.exp(b)).astype(BF16)
    kd = (k * jnp.exp(b_last - b)).astype(BF16)
    e_last = jnp.exp(b_last)
    o = [_dot_nt(qg[:, ksl[hd]], st[hd].astype(BF16)) for hd in heads]
    for hd in heads:
        vt = v_ref[:, vsl[hd]].astype(F32).T.astype(BF16)
        st_ref[hd] = st[hd] * e_last[:, ksl[hd]] + _dot(vt, kd[:, ksl[hd]])

    q16 = q.astype(BF16)
    k16 = k.astype(BF16)
    scores = [jnp.where(eye, _dot_nt(q16[:, ksl[hd]], k16[:, ksl[hd]]), 0.0) for hd in heads]
    h = c // 2
    while h >= 1:
        anchor = _level_anchor(b, h)
        second = (_iota(b.shape, 0) % (2 * h)) >= h
        decay = jnp.exp(jnp.where(second, b - anchor, anchor - b))
        qt = jnp.where(second, q * decay, 0.0).astype(BF16)
        kt = jnp.where(second, 0.0, k * decay).astype(BF16)
        same_block = (row // (2 * h)) == (col // (2 * h))
        for hd in heads:
            p = _dot_nt(qt[:, ksl[hd]], kt[:, ksl[hd]])
            if 2 * h < c:
                p = jnp.where(same_block, p, 0.0)
            scores[hd] = scores[hd] + p
        h //= 2

    for hd in heads:
        oh = o[hd] + _dot(scores[hd].astype(BF16), v_ref[:, vsl[hd]])
        g = g_ref[:, vsl[hd]].astype(F32)
        ms = jnp.mean(oh * oh, axis=-1, keepdims=True)
        o_ref[:, vsl[hd]] = (oh * lax.rsqrt(ms + EPS) * ng_ref[...] * _silu(g)).astype(o_ref.dtype)


def _gla(slab, b_all, norm_g, bsz, s_len):
    tr = min(GLA_CHUNKS_PER_STEP * CHUNK, s_len)
    nc = s_len // tr
    rb = lambda b, i: b * nc + i
    return pl.pallas_call(
        _gla_kernel,
        grid=(bsz, nc),
        in_specs=[
            pl.BlockSpec((tr, GLA_KEY_DIM), lambda b, i: (rb(b, i), COL_GLA_Q // GLA_KEY_DIM)),
            pl.BlockSpec((tr, GLA_KEY_DIM), lambda b, i: (rb(b, i), COL_GLA_K // GLA_KEY_DIM)),
            pl.BlockSpec((tr, GLA_VAL_DIM), lambda b, i: (rb(b, i), COL_GLA_V // GLA_VAL_DIM)),
            pl.BlockSpec((tr, GLA_VAL_DIM), lambda b, i: (rb(b, i), COL_GLA_G // GLA_VAL_DIM)),
            pl.BlockSpec((tr, GLA_KEY_DIM), lambda b, i: (rb(b, i), 0)),
            pl.BlockSpec((1, GLA_DV), lambda b, i: (0, 0)),
        ],
        out_specs=pl.BlockSpec((tr, GLA_VAL_DIM), lambda b, i: (rb(b, i), 0)),
        out_shape=jax.ShapeDtypeStruct((bsz * s_len, GLA_VAL_DIM), BF16),
        scratch_shapes=[pltpu.VMEM((GLA_HEADS, GLA_DV, GLA_DK), F32)],
        compiler_params=_cparams(("parallel", "arbitrary")),
        name="gla",
    )(slab, slab, slab, slab, b_all, norm_g)


def _ssd_kernel(xbc_ref, z_ref, dt_ref, acum_ref, cw_ref, cb_ref, dexp_ref, ng_ref,
                expand_ref, pairsel_ref, shift_ref, o_ref, xpad_ref, st_ref, yd_ref):
    pad = SSM_CONV_PAD
    tr = xbc_ref.shape[0]

    @pl.when(pl.program_id(1) == 0)
    def _():
        st_ref[...] = jnp.zeros_like(st_ref)
        xpad_ref[0:pad, :] = jnp.zeros((pad, SSM_CONV_DIM), BF16)

    xpad_ref[pad:pad + tr, :] = xbc_ref[...]
    for ci in range(tr // CHUNK):
        rows = slice(ci * CHUNK, (ci + 1) * CHUNK)
        _ssd_chunk(xpad_ref.at[ci * CHUNK:(ci + 1) * CHUNK + pad], xbc_ref.at[rows], z_ref.at[rows],
                   dt_ref.at[rows], acum_ref.at[rows], cw_ref, cb_ref, dexp_ref, ng_ref,
                   expand_ref, pairsel_ref, shift_ref, o_ref.at[rows], st_ref, yd_ref.at[rows])
    xpad_ref[0:pad, :] = xpad_ref[tr:tr + pad, :]


def _ssd_chunk(xwin_ref, xbc_ref, z_ref, dt_ref, acum_ref, cw_ref, cb_ref, dexp_ref, ng_ref,
               expand_ref, pairsel_ref, shift_ref, o_ref, st_ref, yd_ref):
    c = CHUNK
    n = SSM_STATE
    gw = SSM_GROUP_WIDTH

    delayed = _dot(shift_ref[...], xwin_ref[...])
    conv = cb_ref[...] + xbc_ref[...].astype(F32) * cw_ref[SSM_CONV_W - 1:SSM_CONV_W, :]
    for i in range(SSM_CONV_W - 1):
        conv = conv + delayed[i * c:(i + 1) * c] * cw_ref[i:i + 1, :]
    xc = _silu(conv)
    xs = xc[:, :SSM_D_INNER]

    a_cum = acum_ref[...]
    both = _dot_sel_rhs2(jnp.concatenate([dt_ref[...], a_cum], axis=0), expand_ref[...])
    dt_e = both[:c]
    a_e = both[c:]
    a_last_e = a_e[c - 1:c, :]
    xdt = xs * dt_e
    decay_out = jnp.exp(a_e)
    xdtd = (xdt * jnp.exp(a_last_e - a_e)).astype(BF16)
    chunk_decay = jnp.exp(a_last_e)

    lane = _iota((c, LANES), 1)
    z0 = jnp.concatenate([jnp.where(lane % 2 == 0, a_cum, 0.0),
                          jnp.where(lane % 2 == 1, a_cum, 0.0)], axis=0)
    pairs = _dot_nt_sel_lhs2(pairsel_ref[...], z0)

    lane2 = _iota((c, 2 * c), 1)
    row2 = _iota((c, 2 * c), 0)
    causal2 = (lane2 % c) <= row2
    bd_row = _iota((2 * c, 2 * c), 0)
    bd_col = _iota((2 * c, 2 * c), 1)
    bd_mask = (bd_row // c) == (bd_col // c)

    for g in range(SSM_GROUPS):
        bg = xc[:, SSM_D_INNER + g * n:SSM_D_INNER + (g + 1) * n]
        cg = xc[:, SSM_D_INNER + SSM_BC_WIDTH + g * n:SSM_D_INNER + SSM_BC_WIDTH + (g + 1) * n]
        cg16 = cg.astype(BF16)
        bg16 = bg.astype(BF16)
        gsl = slice(g * gw, (g + 1) * gw)
        st = st_ref[g]

        y_off = _dot(cg16, st.astype(BF16)) * decay_out[:, gsl]
        st_ref[g] = st * chunk_decay[:, gsl] + _dot(bg.T.astype(BF16), xdtd[:, gsl])

        cb2 = _dot_nt(cg16, jnp.concatenate([bg16, bg16], axis=0))
        for pr in range(SSM_HEADS_PER_GROUP // 2):
            pidx = g * (SSM_HEADS_PER_GROUP // 2) + pr
            lsl = slice(pidx * 2 * c, (pidx + 1) * 2 * c)
            seg = a_e[:, lsl] - pairs[pidx:pidx + 1, :]
            lmat = jnp.exp(jnp.where(causal2, seg, MASK_VALUE))
            w = (cb2 * lmat).astype(BF16)
            xp = xdt[:, lsl]
            bd = jnp.where(bd_mask, jnp.concatenate([xp, xp], axis=0), 0.0).astype(BF16)
            yd_ref[:, lsl] = _dot(w, bd)

        y = yd_ref[:, gsl] + y_off + xs[:, gsl] * dexp_ref[:, gsl]
        y = y * _silu(z_ref[:, gsl].astype(F32))
        ms = jnp.mean(y * y, axis=-1, keepdims=True)
        o_ref[:, gsl] = (y * lax.rsqrt(ms + EPS) * ng_ref[:, gsl]).astype(o_ref.dtype)


def _ssd(slab, dt, a_cum, conv_w, conv_b, d_exp, norm_g, expand, pairsel, shift, bsz, s_len):
    tr = min(SSD_CHUNKS_PER_STEP * CHUNK, s_len)
    nc = s_len // tr
    rb = lambda b, i: b * nc + i
    const = lambda b, i: (0, 0)
    return pl.pallas_call(
        _ssd_kernel,
        grid=(bsz, nc),
        in_specs=[
            pl.BlockSpec((tr, SSM_CONV_DIM), lambda b, i: (rb(b, i), COL_XBC // SSM_CONV_DIM)),
            pl.BlockSpec((tr, SSM_D_INNER), lambda b, i: (rb(b, i), COL_SSM_Z // SSM_D_INNER)),
            pl.BlockSpec((tr, LANES), lambda b, i: (rb(b, i), 0)),
            pl.BlockSpec((tr, LANES), lambda b, i: (rb(b, i), 0)),
            pl.BlockSpec((SSM_CONV_W, SSM_CONV_DIM), const),
            pl.BlockSpec((1, SSM_CONV_DIM), const),
            pl.BlockSpec((1, SSM_D_INNER), const),
            pl.BlockSpec((1, SSM_D_INNER), const),
            pl.BlockSpec((LANES, SSM_D_INNER), const),
            pl.BlockSpec((SSM_HEADS // 2, LANES), const),
            pl.BlockSpec(((SSM_CONV_W - 1) * CHUNK, CHUNK + SSM_CONV_PAD), const),
        ],
        out_specs=pl.BlockSpec((tr, SSM_D_INNER), lambda b, i: (rb(b, i), 0)),
        out_shape=jax.ShapeDtypeStruct((bsz * s_len, SSM_D_INNER), BF16),
        scratch_shapes=[
            pltpu.VMEM((tr + SSM_CONV_PAD, SSM_CONV_DIM), BF16),
            pltpu.VMEM((SSM_GROUPS, SSM_STATE, SSM_GROUP_WIDTH), F32),
            pltpu.VMEM((tr, SSM_D_INNER), F32),
        ],
        compiler_params=_cparams(("parallel", "arbitrary")),
        name="ssd",
    )(slab, slab, dt, a_cum, conv_w, conv_b, d_exp, norm_g, expand, pairsel, shift)


def _rope_table_kernel(pos_ref, freq_ref, c_ref, s1_ref, s2_ref):
    ang = pos_ref[...].astype(F32) * freq_ref[...]
    d = _iota(ang.shape, 1) % DIFF_HEAD_DIM
    half = ROT_DIM // 2
    cosv = jnp.cos(ang)
    sinv = jnp.sin(ang)
    c_ref[...] = jnp.where(d < ROT_DIM, cosv, 1.0)
    s1_ref[...] = jnp.where(d < half, -sinv, 0.0)
    s2_ref[...] = jnp.where((d >= half) & (d < ROT_DIM), sinv, 0.0)


def _rope_tables(pos_col, freq_row, tr=512):
    m = pos_col.shape[0]
    spec = pl.BlockSpec((tr, LANES), lambda i: (i, 0))
    shp = jax.ShapeDtypeStruct((m, LANES), F32)
    return pl.pallas_call(
        _rope_table_kernel,
        grid=(m // tr,),
        in_specs=[pl.BlockSpec((tr, 1), lambda i: (i, 0)), pl.BlockSpec((1, LANES), lambda i: (0, 0))],
        out_specs=[spec, spec, spec],
        out_shape=[shp, shp, shp],
        compiler_params=_cparams(("parallel",)),
        name="rope_tables",
    )(pos_col, freq_row)


def _attn_prep_kernel(q_ref, k_ref, v_ref, c_ref, s1_ref, s2_ref, qt_ref, ko_ref, vt_ref):
    half = ROT_DIM // 2
    cosf = c_ref[...]
    s1 = s1_ref[...]
    s2 = s2_ref[...]

    tile_rows = DIFF_V_ROWS - DIFF_V_DIM
    ones_tile = jnp.where(_iota((tile_rows, q_ref.shape[0]), 0) == 0, 1.0, 0.0).astype(BF16)

    def rope(t):
        return t * cosf + pltpu.roll(t, LANES - half, 1) * s1 + pltpu.roll(t, half, 1) * s2

    for hd in range(DIFF_HEADS):
        sl = slice(hd * LANES, (hd + 1) * LANES)
        q = q_ref[:, sl].astype(F32)
        k = k_ref[:, sl].astype(F32)
        qt_ref[0, hd, 0] = (rope(q) * QK_SCALE_LOG2).T.astype(BF16)
        ko_ref[:, sl] = rope(k).astype(BF16)
        vt_ref[0, hd, 0, 0:DIFF_V_DIM, :] = v_ref[:, sl].astype(F32).T.astype(BF16)
        vt_ref[0, hd, 0, DIFF_V_DIM:DIFF_V_ROWS, :] = ones_tile


def _attn_prep(slab, cosf, s1, s2, bsz, s_len, tr):
    m = bsz * s_len
    nb = s_len // tr
    w = DIFF_QK_WIDTH
    row = pl.BlockSpec((tr, w), lambda i: (i, 0))
    tab = pl.BlockSpec((tr, LANES), lambda i: (i, 0))
    tposed = lambda rows: pl.BlockSpec((1, DIFF_HEADS, 1, rows, tr), lambda i: (i // nb, 0, i % nb, 0, 0))
    return pl.pallas_call(
        _attn_prep_kernel,
        grid=(m // tr,),
        in_specs=[
            pl.BlockSpec((tr, w), lambda i: (i, COL_DIFF_Q // w)),
            pl.BlockSpec((tr, w), lambda i: (i, COL_DIFF_K // w)),
            pl.BlockSpec((tr, w), lambda i: (i, COL_DIFF_V // w)),
            tab, tab, tab,
        ],
        out_specs=[tposed(LANES), row, tposed(DIFF_V_ROWS)],
        out_shape=[
            jax.ShapeDtypeStruct((bsz, DIFF_HEADS, nb, LANES, tr), BF16),
            jax.ShapeDtypeStruct((m, w), BF16),
            jax.ShapeDtypeStruct((bsz, DIFF_HEADS, nb, DIFF_V_ROWS, tr), BF16),
        ],
        compiler_params=_cparams(("parallel",)),
        name="attn_prep",
    )(slab, slab, slab, cosf, s1, s2)


def _flash_kernel(q_ref, qn_ref, k_ref, vt_ref, lq1_ref, lk1_ref, lq2_ref, lk2_ref, ng_ref, o_ref,
                  sa_ref, sb_ref, sc_ref, *, tq, cw, lambda_init):
    qi = pl.program_id(2)
    nch = 2 * tq // cw
    lead = 1

    def chains(ref):
        qt = ref[0, 0, 0]
        feat = _iota(qt.shape, 0)
        zero = jnp.zeros_like(qt)
        qs = jnp.concatenate([jnp.where(feat < DIFF_HEAD_DIM, qt, zero),
                              jnp.where(feat >= DIFF_HEAD_DIM, qt, zero)], axis=1)
        return [qs[:, c * cw:(c + 1) * cw] for c in range(nch)]

    q_chains = chains(q_ref)

    def keys(j):
        return k_ref[pl.ds(pl.multiple_of(j * tq, tq), tq), :]

    def scores(j, s_ref, qc):
        kj = keys(j)
        for c in range(nch):
            s_ref[c] = _dot(kj, qc[c])

    def absorb(j, s_ref, stats, masked, nxt_ref=None):
        vtj = vt_ref[0, 0, j]
        kn = None if nxt_ref is None else keys(j + 1)
        out = []
        if nxt_ref is not None:
            for c in range(min(lead, nch)):
                nxt_ref[c] = _dot(kn, q_chains[c])
        for c in range(nch):
            if nxt_ref is not None and c + lead < nch:
                nxt_ref[c + lead] = _dot(kn, q_chains[c + lead])
            m, acc = stats[c]
            if masked:
                q0 = (c * cw) % tq
                nk = q0 + cw
                kc = _iota((nk // CHUNK, 1, cw), 0)
                qc = (q0 + _iota((nk // CHUNK, 1, cw), 2)) // CHUNK
                bias = jnp.where(kc <= qc, 0.0, MASK_VALUE)
                s = (s_ref[c, 0:nk, :].reshape(nk // CHUNK, CHUNK, cw) + bias).reshape(nk, cw)
                vt = vtj[:, 0:nk]
            else:
                s = s_ref[c]
                vt = vtj
            m_new = jnp.maximum(m, jnp.max(s, axis=0, keepdims=True))
            p = jnp.exp2(s - m_new).astype(BF16)
            alpha = jnp.exp2(m - m_new)
            acc = acc * alpha + _dot(vt, p)
            out.append((m_new, acc))
        return tuple(out)

    def score_next_block():
        scores(0, sc_ref, chains(qn_ref))

    def pair(j, stats):
        stats = absorb(j, sa_ref, stats, False, sb_ref)
        return absorb(j + 1, sb_ref, stats, False, sa_ref)

    def quad(jj, stats):
        return pair(4 * jj + 3, pair(4 * jj + 1, stats))

    def last_even(stats):
        score_next_block()
        return absorb(qi, sa_ref, stats, True)

    def last_odd(stats):
        stats = absorb(qi - 1, sa_ref, stats, False, sb_ref)
        score_next_block()
        return absorb(qi, sb_ref, stats, True)

    init = tuple((jnp.full((1, cw), MASK_VALUE, F32), jnp.zeros((DIFF_V_ROWS, cw), F32))
                 for _ in range(nch))

    def first_block(stats):
        scores(0, sc_ref, q_chains)
        stats = absorb(0, sc_ref, stats, True)
        score_next_block()
        return stats

    def later_block(stats):
        rest = qi - 1
        stats = absorb(0, sc_ref, stats, False, sa_ref)
        stats = lax.fori_loop(0, rest // 4, quad, stats)
        stats = lax.cond(rest % 4 >= 2, lambda st: pair(4 * (rest // 4) + 1, st), lambda st: st, stats)
        return lax.cond(rest % 2 == 1, last_odd, last_even, stats)

    carry = lax.cond(qi == 0, first_block, later_block, init)

    lam = (jnp.exp(jnp.sum(lq1_ref[...] * lk1_ref[...], keepdims=True))
           - jnp.exp(jnp.sum(lq2_ref[...] * lk2_ref[...], keepdims=True)) + lambda_init)
    on = jnp.concatenate([acc[0:DIFF_V_DIM] / acc[DIFF_V_DIM:DIFF_V_DIM + 1] for (_, acc) in carry],
                         axis=1)
    o = on[:, :tq] - lam * on[:, tq:]
    ms = jnp.mean(o * o, axis=0, keepdims=True)
    o = o * lax.rsqrt(ms + DIFF_SUBLN_EPS)
    o_ref[...] = (o.T * ng_ref[...] * (1.0 - lambda_init)).astype(o_ref.dtype)


def _flash(q_t, k_r, v_t, lq1, lk1, lq2, lk2, norm_g, bsz, s_len, tq, lambda_init):
    nq = s_len // tq
    cw = min(256, tq)
    vec = pl.BlockSpec((1, DIFF_HEAD_DIM), lambda b, h, i: (0, 0))
    return pl.pallas_call(
        functools.partial(_flash_kernel, tq=tq, cw=cw, lambda_init=lambda_init),
        grid=(bsz, DIFF_HEADS, nq),
        in_specs=[
            pl.BlockSpec((1, 1, 1, LANES, tq), lambda b, h, i: (b, h, i, 0, 0)),
            pl.BlockSpec((1, 1, 1, LANES, tq), lambda b, h, i: (b, h, jnp.minimum(i + 1, nq - 1), 0, 0)),
            pl.BlockSpec((s_len, LANES), lambda b, h, i: (b, h)),
            pl.BlockSpec((1, 1, nq, DIFF_V_ROWS, tq), lambda b, h, i: (b, h, 0, 0, 0)),
            vec, vec, vec, vec,
            pl.BlockSpec((1, DIFF_V_DIM), lambda b, h, i: (0, 0)),
        ],
        out_specs=pl.BlockSpec((tq, DIFF_V_DIM), lambda b, h, i: (b * nq + i, h)),
        out_shape=jax.ShapeDtypeStruct((bsz * s_len, DIFF_V_WIDTH), BF16),
        scratch_shapes=[pltpu.VMEM((2 * tq // cw, tq, cw), F32)] * 3,
        compiler_params=_cparams(("parallel", "parallel", "arbitrary")),
        name="flash",
    )(q_t, q_t, k_r, v_t, lq1, lk1, lq2, lk2, norm_g)


def _merge_kernel(x_ref, yg_ref, ys_ref, yd_ref, g0_ref, g1_ref, g2_ref, bg_ref,
                  wg_ref, ws_ref, wd_ref, wo_ref, o_ref):
    def gate(g_ref, i):
        return 1.0 / (1.0 + jnp.exp(-(g_ref[...].astype(F32) + bg_ref[i:i + 1, :])))

    mixed = gate(g0_ref, 0) * _dot(yg_ref[...], wg_ref[...])
    mixed = mixed + gate(g1_ref, 1) * _dot(ys_ref[...], ws_ref[...])
    mixed = mixed + gate(g2_ref, 2) * _dot(yd_ref[...], wd_ref[...])
    o_ref[...] = x_ref[...] + _dot(mixed.astype(BF16), wo_ref[...])


def _merge(x2, y_gla, y_ssm, y_diff, slab, b_gate, w_g, w_s, w_d, w_o, layer, tm=512):
    m = x2.shape[0]
    gcol = COL_GATE // D_MODEL
    row = lambda w: pl.BlockSpec((tm, w), lambda i: (i, 0))
    const = lambda r, c: pl.BlockSpec((r, c), lambda i: (0, 0))
    weight = lambda r, c: pl.BlockSpec((None, r, c), lambda i: (layer, 0, 0))
    return pl.pallas_call(
        _merge_kernel,
        grid=(m // tm,),
        in_specs=[
            row(D_MODEL), row(GLA_VAL_DIM), row(SSM_D_INNER), row(DIFF_V_WIDTH),
            pl.BlockSpec((tm, D_MODEL), lambda i: (i, gcol)),
            pl.BlockSpec((tm, D_MODEL), lambda i: (i, gcol + 1)),
            pl.BlockSpec((tm, D_MODEL), lambda i: (i, gcol + 2)),
            const(N_BRANCHES, D_MODEL),
            weight(GLA_VAL_DIM, D_MODEL), weight(SSM_D_INNER, D_MODEL), weight(DIFF_V_WIDTH, D_MODEL),
            weight(D_MODEL, D_MODEL),
        ],
        out_specs=row(D_MODEL),
        out_shape=jax.ShapeDtypeStruct((m, D_MODEL), F32),
        compiler_params=_cparams(("parallel",)),
        name="merge",
    )(x2, y_gla, y_ssm, y_diff, slab, slab, slab, b_gate, w_g, w_s, w_d, w_o)


def _mlp_kernel(x_ref, g_ref, wu_ref, wd_ref, gf_ref, o_ref, *, final_norm, fc):
    x = x_ref[...]
    ms = jnp.mean(x * x, axis=-1, keepdims=True)
    h = (x * lax.rsqrt(ms + EPS) * g_ref[...]).astype(BF16)
    acc = x
    for c0 in range(0, D_FF, fc):
        u = jnp.maximum(_dot(h, wu_ref[:, c0:c0 + fc]), 0.0)
        acc = acc + _dot((u * u).astype(BF16), wd_ref[c0:c0 + fc, :])
    if final_norm:
        ms = jnp.mean(acc * acc, axis=-1, keepdims=True)
        acc = acc * lax.rsqrt(ms + EPS) * gf_ref[...]
    o_ref[...] = acc


def _mlp(x2, g, w_up, w_down, g_final, final_norm, layer, tm=512, fc=1024):
    m = x2.shape[0]
    row = pl.BlockSpec((tm, D_MODEL), lambda i: (i, 0))
    const = lambda r, c: pl.BlockSpec((r, c), lambda i: (0, 0))
    weight = lambda r, c: pl.BlockSpec((None, r, c), lambda i: (layer, 0, 0))
    return pl.pallas_call(
        functools.partial(_mlp_kernel, final_norm=final_norm, fc=fc),
        grid=(m // tm,),
        in_specs=[row, const(1, D_MODEL), weight(D_MODEL, D_FF), weight(D_FF, D_MODEL), const(1, D_MODEL)],
        out_specs=row,
        out_shape=jax.ShapeDtypeStruct((m, D_MODEL), F32),
        compiler_params=_cparams(("parallel",)),
        name="mlp",
    )(x2, g, w_up, w_down, g_final)


def _slab_weights(w_in):
    sizes = (GLA_KEY_DIM, GLA_KEY_DIM, GLA_VAL_DIM, GLA_LOW_RANK, GLA_VAL_DIM,
             SSM_D_INNER, SSM_CONV_DIM, SSM_HEADS,
             DIFF_QK_WIDTH, DIFF_QK_WIDTH, DIFF_V_WIDTH, N_BRANCHES * D_MODEL)
    offs = np.concatenate([[0], np.cumsum(sizes)])
    (a_q, a_k, a_v, a_gk, a_g, b_z, b_xbc, b_dt, c_q, c_k, c_v, gate) = [
        w_in[:, :, int(offs[i]):int(offs[i + 1])] for i in range(len(sizes))]
    slab = jnp.concatenate([b_xbc, gate, a_v, a_g, c_q, c_k, c_v, a_q, a_k, b_z], axis=2).astype(BF16)
    zeros = lambda n: jnp.zeros((w_in.shape[0], D_MODEL, n), w_in.dtype)
    small = jnp.concatenate([a_gk, zeros(LANES - GLA_LOW_RANK), b_dt, zeros(LANES - SSM_HEADS)],
                            axis=2).astype(BF16)
    return slab, small


def _pad_lanes(v, fill=0.0):
    v = v.reshape(1, -1).astype(F32)
    return jnp.pad(v, ((0, 0), (0, LANES - v.shape[1])), constant_values=fill)


def _forward(x, positions, norm_mix_g, w_in, b_gate, gla_w_gk2, gla_b_gk, gla_norm_g,
             ssm_conv_w, ssm_conv_b, ssm_dt_bias, ssm_a_log, ssm_d, ssm_norm_g,
             diff_lq1, diff_lk1, diff_lq2, diff_lk2, diff_norm_g,
             w_br_gla, w_br_ssm, w_br_diff, w_out, norm_mlp_g, w_mlp_up, w_mlp_down,
             norm_final_g, tq):
    bsz, s_len, _ = x.shape
    m = bsz * s_len
    depth = w_in.shape[0]
    x2 = x.reshape(m, D_MODEL).astype(F32)

    head_of_lane = np.arange(SSM_D_INNER) // SSM_HEAD_DIM
    expand = jnp.asarray((np.arange(LANES)[:, None] == head_of_lane[None, :]), BF16)
    pairsel = jnp.asarray((np.arange(LANES)[None, :] // 2) == np.arange(SSM_HEADS // 2)[:, None], BF16)
    pairsel = pairsel * jnp.asarray(np.arange(LANES)[None, :] < SSM_HEADS, BF16)
    t = np.arange((SSM_CONV_W - 1) * CHUNK)
    src_row = SSM_CONV_PAD + (t % CHUNK) - (SSM_CONV_W - 1) + (t // CHUNK)
    shift = jnp.asarray(np.arange(CHUNK + SSM_CONV_PAD)[None, :] == src_row[:, None], BF16)
    d = np.arange(LANES) % DIFF_HEAD_DIM
    inv_freq = ROPE_THETA ** (-(2.0 * (d % (ROT_DIM // 2))) / ROT_DIM)
    freq_row = jnp.asarray(np.where(d < ROT_DIM, inv_freq, 0.0).reshape(1, LANES), F32)

    cosf, s1, s2 = _rope_tables(positions.reshape(m, 1), freq_row, tr=min(512, s_len))

    w_slab, w_small = _slab_weights(w_in)
    wb_gla, wb_ssm, wb_diff, wb_out = (w.astype(BF16) for w in (w_br_gla, w_br_ssm, w_br_diff, w_out))
    wb_up, wb_down = w_mlp_up.astype(BF16), w_mlp_down.astype(BF16)

    for l in range(depth):
        slab, small = _inproj(x2, norm_mix_g[l].reshape(1, -1), w_slab, w_small, l,
                              tm=min(1024, m))

        wgk_pad = jnp.pad(gla_w_gk2[l].astype(F32), ((0, LANES - GLA_LOW_RANK), (0, 0)))
        b_all, dt, a_cum = _gates(small, wgk_pad, gla_b_gk[l].reshape(1, -1),
                                  _pad_lanes(ssm_dt_bias[l]), _pad_lanes(ssm_a_log[l]))
        y_gla = _gla(slab, b_all, gla_norm_g[l].reshape(1, -1), bsz, s_len)

        d_exp = jnp.repeat(ssm_d[l].astype(F32), SSM_HEAD_DIM).reshape(1, -1)
        y_ssm = _ssd(slab, dt, a_cum, ssm_conv_w[l], ssm_conv_b[l].reshape(1, -1), d_exp,
                     ssm_norm_g[l].reshape(1, -1), expand, pairsel, shift, bsz, s_len)

        lambda_init = 0.8 - 0.6 * math.exp(-0.3 * l)
        q_t, k_r, v_t = _attn_prep(slab, cosf, s1, s2, bsz, s_len, tq)
        y_diff = _flash(q_t, k_r, v_t, diff_lq1[l].reshape(1, -1), diff_lk1[l].reshape(1, -1),
                        diff_lq2[l].reshape(1, -1), diff_lk2[l].reshape(1, -1),
                        diff_norm_g[l].reshape(1, -1), bsz, s_len, tq, lambda_init)

        x2 = _merge(x2, y_gla, y_ssm, y_diff, slab, b_gate[l].reshape(N_BRANCHES, D_MODEL),
                    wb_gla, wb_ssm, wb_diff, wb_out, l, tm=min(512, m))
        x2 = _mlp(x2, norm_mlp_g[l].reshape(1, -1), wb_up, wb_down, norm_final_g.reshape(1, -1),
                  final_norm=(l == depth - 1), layer=l, tm=min(512, m))
    return x2.reshape(bsz, s_len, D_MODEL)


def kernel(x, positions, norm_mix_g, w_in, b_gate, gla_w_gk2, gla_b_gk, gla_norm_g, ssm_conv_w,
           ssm_conv_b, ssm_dt_bias, ssm_a_log, ssm_d, ssm_norm_g, diff_lq1, diff_lk1, diff_lq2,
           diff_lk2, diff_norm_g, w_br_gla, w_br_ssm, w_br_diff, w_out, norm_mlp_g, w_mlp_up,
           w_mlp_down, norm_final_g):
    return _forward(x, positions, norm_mix_g, w_in, b_gate, gla_w_gk2, gla_b_gk, gla_norm_g,
                    ssm_conv_w, ssm_conv_b, ssm_dt_bias, ssm_a_log, ssm_d, ssm_norm_g,
                    diff_lq1, diff_lk1, diff_lq2, diff_lk2, diff_norm_g,
                    w_br_gla, w_br_ssm, w_br_diff, w_out, norm_mlp_g, w_mlp_up, w_mlp_down,
                    norm_final_g, tq=min(512, x.shape[1]))
```
